```python
import jax, jax.numpy as jnp
from jax import lax
import numpy as np

D_MODEL = 1024
BATCH = 8
SEQ = 4096
DEPTH = 1

N_META = 16
D_CONV = D_MODEL
CONV_WIDTH = 31
HG_HEADS = 8
HG_DK = 128
HG_DV = D_MODEL // HG_HEADS
D_HK = HG_HEADS * HG_DK
D_HV = HG_HEADS * HG_DV
CHUNK = 64
EPS = 1e-6
SPLIT_SIZES = (D_CONV, D_CONV, D_CONV, D_HK, D_HK, D_HV, D_HV, D_MODEL, D_MODEL)
D_IN = D_CONV * 3 + D_HK * 2 + D_HV * 2 + D_MODEL * 2

kernel_name = "hybrid_conformer_hgrn2_gated_block"


def rmsnorm(x, g):
    xf = x.astype(jnp.float32)
    y = xf * lax.rsqrt(jnp.mean(xf * xf, axis=-1, keepdims=True) + EPS)
    return (y * g.astype(jnp.float32)).astype(x.dtype)


def layernorm(x, g, b):
    xf = x.astype(jnp.float32)
    mu = jnp.mean(xf, axis=-1, keepdims=True)
    var = jnp.mean(jnp.square(xf - mu), axis=-1, keepdims=True)
    y = (xf - mu) * lax.rsqrt(var + EPS)
    return (y * g.astype(jnp.float32) + b.astype(jnp.float32)).astype(x.dtype)


def conformer_branch(u_a, u_b, z, conv_w, conv_b, ln_g, ln_b, w_out):
    a = u_a * jax.nn.sigmoid(u_b)
    c = lax.conv_general_dilated(
        a, conv_w[:, None, :].astype(a.dtype), window_strides=(1,),
        padding=[(CONV_WIDTH - 1, 0)],
        dimension_numbers=('NWC', 'WIO', 'NWC'),
        feature_group_count=D_CONV) + conv_b
    c = jax.nn.silu(layernorm(c, ln_g, ln_b))
    return (c * jax.nn.silu(z)) @ w_out


def hgrn2_branch(q_raw, f_raw, i_raw, g, lb, gnorm_g, w_out):
    bsz, seqlen, _ = q_raw.shape
    out_dtype = i_raw.dtype
    q = jax.nn.silu(q_raw.astype(jnp.float32))
    f = lb + (1.0 - lb) * jax.nn.sigmoid(f_raw.astype(jnp.float32))
    log_f = jnp.log(f)
    k = 1.0 - f
    v = i_raw.astype(jnp.float32)
    pad = (-seqlen) % CHUNK
    n_chunks = (seqlen + pad) // CHUNK

    def to_chunks(t, d):
        t = jnp.pad(t, ((0, 0), (pad, 0), (0, 0)))
        t = t.reshape(bsz, n_chunks, CHUNK, HG_HEADS, d)
        return jnp.transpose(t, (1, 0, 3, 2, 4))

    qc = to_chunks(q, HG_DK)
    kc = to_chunks(k, HG_DK)
    vc = to_chunks(v, HG_DV)
    bc = jnp.cumsum(to_chunks(log_f, HG_DK), axis=3)
    causal = jnp.tril(jnp.ones((CHUNK, CHUNK), dtype=bool))[None, None, :, :, None]

    def step(S, inp):
        qi, ki, vi, bi = inp
        o_inter = jnp.einsum('bhtk,bhkv->bhtv', qi * jnp.exp(bi), S)
        diff = bi[:, :, :, None, :] - bi[:, :, None, :, :]
        decay = jnp.exp(jnp.where(causal, diff, -jnp.inf))
        attn = jnp.einsum('bhtk,bhsk,bhtsk->bhts', qi, ki, decay)
        o_intra = jnp.einsum('bhts,bhsv->bhtv', attn, vi)
        b_last = bi[:, :, -1:, :]
        S_new = jnp.exp(b_last[:, :, 0, :])[..., None] * S + jnp.einsum(
            'bhsk,bhsv->bhkv', ki * jnp.exp(b_last - bi), vi)
        return S_new, o_inter + o_intra

    S0 = jnp.zeros((bsz, HG_HEADS, HG_DK, HG_DV), jnp.float32)
    _, o = lax.scan(step, S0, (qc, kc, vc, bc))
    o = jnp.transpose(o, (1, 0, 3, 2, 4)).reshape(bsz, n_chunks * CHUNK, HG_HEADS, HG_DV)
    o = o[:, pad:]
    o = o * lax.rsqrt(jnp.mean(o * o, axis=-1, keepdims=True) + EPS)
    o = o * gnorm_g.astype(jnp.float32).reshape(HG_HEADS, HG_DV)
    o = o.reshape(bsz, seqlen, D_HV).astype(out_dtype)
    return (o * jax.nn.silu(g)) @ w_out


def setup_inputs(seed: int = 0) -> dict:
    key = jax.random.key(seed)
    ks = jax.random.split(key, 16)
    f32 = jnp.float32
    nrm = lambda k, shape, s: jax.random.normal(k, shape, f32) * s
    return {
        "x": nrm(ks[0], (BATCH, SEQ, D_MODEL), 1.0),
        "meta_tokens": nrm(ks[1], (N_META, D_MODEL), 1.0),
        "norm_g": 1.0 + nrm(ks[2], (DEPTH, D_MODEL), 0.02),
        "w_in": nrm(ks[3], (DEPTH, D_MODEL, D_IN), D_MODEL ** -0.5),
        "conv_w": nrm(ks[4], (DEPTH, CONV_WIDTH, D_CONV), CONV_WIDTH ** -0.5),
        "conv_b": nrm(ks[5], (DEPTH, D_CONV), 0.02),
        "ln_g": 1.0 + nrm(ks[6], (DEPTH, D_CONV), 0.02),
        "ln_b": nrm(ks[7], (DEPTH, D_CONV), 0.02),
        "w_conv_out": nrm(ks[8], (DEPTH, D_CONV, D_MODEL), D_CONV ** -0.5),
        "lb_logits": nrm(ks[9], (DEPTH + 1, D_HK), 0.5),
        "gnorm_g": 1.0 + nrm(ks[10], (DEPTH, D_HV), 0.02),
        "w_rec_out": nrm(ks[11], (DEPTH, D_HV, D_MODEL), D_HV ** -0.5),
        "w_out": nrm(ks[12], (DEPTH, D_MODEL, D_MODEL), D_MODEL ** -0.5),
        "final_g": 1.0 + nrm(ks[13], (D_MODEL,), 0.02),
    }


def reference(x, meta_tokens, norm_g, w_in, conv_w, conv_b, ln_g, ln_b, w_conv_out,
              lb_logits, gnorm_g, w_rec_out, w_out, final_g):
    bsz = x.shape[0]
    meta = jnp.broadcast_to(meta_tokens.astype(x.dtype)[None], (bsz, N_META, D_MODEL))
    h_res = jnp.concatenate([meta, x], axis=1)
    lb_all = jnp.cumsum(jax.nn.softmax(lb_logits.astype(jnp.float32), axis=0), axis=0)
    split_idx = [int(v) for v in np.cumsum(SPLIT_SIZES)[:-1]]
    for l in range(DEPTH):
        h = rmsnorm(h_res, norm_g[l])
        proj = h @ w_in[l]
        glu_a, glu_b, z_conv, q, f, i, g_rec, m_conv, m_rec = jnp.split(proj, split_idx, axis=-1)
        y_conv = conformer_branch(glu_a, glu_b, z_conv, conv_w[l], conv_b[l],
                                  ln_g[l], ln_b[l], w_conv_out[l])
        y_rec = hgrn2_branch(q, f, i, g_rec, lb_all[l], gnorm_g[l], w_rec_out[l])
        merged = jax.nn.sigmoid(m_conv) * y_conv + jax.nn.sigmoid(m_rec) * y_rec
        h_res = h_res + merged @ w_out[l]
    return rmsnorm(h_res[:, N_META:], final_g)
```

```python
import functools

import numpy as np
import jax
import jax.numpy as jnp
from jax import lax
from jax.experimental import pallas as pl
from jax.experimental.pallas import tpu as pltpu

D_MODEL = 1024
N_META = 16
CONV_WIDTH = 31
HEADS = 8
HEAD_DIM = 128
EPS = 1e-6

SUBLANES = 8
BLOCK_T = 256
N_POS = BLOCK_T // SUBLANES
N_LEVELS = 8
DIAG_LEVEL = N_LEVELS
HALO_ROWS = N_POS * SUBLANES
ROW_TILE = 16
VMEM_LIMIT_BYTES = 56 * 1024 * 1024

F32 = jnp.float32
BF16 = jnp.bfloat16

_C_GLU_A, _C_GLU_B, _C_Z, _C_Q, _C_F, _C_I, _C_G, _C_MC, _C_MR = [i * D_MODEL for i in range(9)]


def _sigmoid(x):
    return jax.nn.sigmoid(x)


def _silu(x):
    return x * jax.nn.sigmoid(x)


def _rows(i, n):
    return pl.ds(pl.multiple_of(i * n, n), n)


def _proj(h_ref, w_ref, col):
    return jnp.dot(h_ref[...], w_ref[:, col:col + D_MODEL], preferred_element_type=F32)


def _sub_roll(x, shift):
    return pltpu.roll(x, shift % SUBLANES, 0)


def _cat_bf16(tiles):
    return jnp.concatenate(tiles, axis=0).astype(BF16)


def _head_levels(q_t, k_t, f_t, want_q):
    fwd = list(f_t)
    rev = [None] * N_POS
    levels = []

    def times(x_t, s_t):
        return [x if s is None else x * s for x, s in zip(x_t, s_t)]

    n = 1
    while n < N_POS:
        levels.append((_cat_bf16(times(q_t, fwd)) if want_q else None, _cat_bf16(times(k_t, rev))))
        new_fwd, new_rev = list(fwd), list(rev)
        for b0 in range(0, N_POS, 2 * n):
            tot_lo, tot_up = fwd[b0 + n - 1], fwd[b0 + 2 * n - 1]
            for i in range(n):
                new_fwd[b0 + n + i] = fwd[b0 + n + i] * tot_lo
                new_rev[b0 + i] = tot_up if rev[b0 + i] is None else rev[b0 + i] * tot_up
        fwd, rev = new_fwd, new_rev
        n *= 2

    g = fwd[N_POS - 1]
    sub = lax.broadcasted_iota(jnp.int32, g.shape, 0)
    sub4 = sub % 4
    one = jnp.ones_like(g)
    gm = [None] + [_sub_roll(g, s) for s in range(1, SUBLANES)]
    gp = [None] + [_sub_roll(g, -s) for s in range(1, SUBLANES)]
    u64 = jnp.where(sub4 == 3, gm[1], one)
    w64 = jnp.where(sub4 == 0, gp[1], one)
    u128 = (jnp.where(sub4 >= 1, gm[1], one) * jnp.where(sub4 >= 2, gm[2], one)
            * jnp.where(sub4 >= 3, gm[3], one))
    w128 = (jnp.where(sub4 <= 2, gp[1], one) * jnp.where(sub4 <= 1, gp[2], one)
            * jnp.where(sub4 == 0, gp[3], one))
    px, sx = one, one
    for s in range(1, SUBLANES):
        px = px * jnp.where(sub >= s, gm[s], one)
        sx = sx * jnp.where(sub <= SUBLANES - 1 - s, gp[s], one)
    g_total = (px * g * sx)[0:1, :]

    k_rev = times(k_t, rev)
    q_fwd = times(q_t, fwd) if want_q else None

    def scaled(x_t, s):
        return _cat_bf16([x * s for x in x_t])

    levels.append((_cat_bf16(q_fwd) if want_q else None, _cat_bf16(k_rev)))
    levels.append((scaled(q_fwd, u64) if want_q else None, scaled(k_rev, w64)))
    levels.append((scaled(q_fwd, u128) if want_q else None, scaled(k_rev, w128)))
    levels.append((_cat_bf16(q_t) if want_q else None, _cat_bf16(k_t)))
    q_inter = scaled(q_fwd, px) if want_q else None
    k_state = scaled(k_rev, sx)
    return levels, q_inter, k_state, g_total


def _block_kernel(*refs, is_meta):
    if is_meta:
        (x_ref, pmat_ref, norm_g_ref, w_in_ref, lb_ref,
         carry_out_ref, st_out_ref,
         h_nat, h_p, buf0, buf1, fh, vh) = refs
        carry, st = carry_out_ref, st_out_ref
    else:
        (x_ref, mcarry_ref, mst_ref, pmat_ref, pmat_t_ref, lvl_ref, norm_g_ref, w_in_ref,
         conv_w_ref, conv_b_ref, ln_g_ref, ln_b_ref, wco_ref, lb_ref, gn_ref, wro_ref,
         wo_ref, fg_ref, out_ref,
         h_nat, h_p, buf0, buf1, buf2, a_ext, carry, st, qh, fh, vh, oh, u_bf) = refs

        @pl.when(pl.program_id(1) == 0)
        def _():
            carry[...] = mcarry_ref[...]
            st[...] = mst_ref[...]

    n_tiles = BLOCK_T // ROW_TILE

    def x_rows(rows):
        return x_ref[rows, :] if is_meta else x_ref[0, rows, :]

    def norm_body(i, c):
        rows = _rows(i, ROW_TILE)
        xv = x_rows(rows)
        ms = jnp.mean(xv * xv, axis=-1, keepdims=True)
        h_nat[rows, :] = (xv * lax.rsqrt(ms + EPS) * norm_g_ref[...]).astype(BF16)
        return c
    lax.fori_loop(0, n_tiles, norm_body, 0)
    h_p[...] = jnp.dot(pmat_ref[...], h_nat[...], preferred_element_type=F32).astype(BF16)

    buf0[...] = _proj(h_p, w_in_ref, _C_GLU_A)
    buf1[...] = _proj(h_p, w_in_ref, _C_GLU_B)

    def glu_body(i, c):
        rows = _rows(i, SUBLANES)
        a = buf0[rows, :] * _sigmoid(buf1[rows, :])
        rolled = pltpu.roll(a, 1, 0)
        if is_meta:
            carry[rows, :] = rolled
        else:
            sub = lax.broadcasted_iota(jnp.int32, a.shape, 0)
            a_ext[pl.ds(pl.multiple_of(HALO_ROWS + i * SUBLANES, SUBLANES), SUBLANES), :] = a
            a_ext[rows, :] = jnp.where(sub == 0, carry[rows, :], rolled)
            carry[rows, :] = rolled
        return c
    lax.fori_loop(0, BLOCK_T // SUBLANES, glu_body, 0)

    if not is_meta:
        conv_rows = 32

        def conv_body(i, c):
            base = i * conv_rows
            for l in range(D_MODEL // HEAD_DIM):
                lanes = slice(l * HEAD_DIM, (l + 1) * HEAD_DIM)
                acc = jnp.broadcast_to(conv_b_ref[:, lanes], (conv_rows, HEAD_DIM))
                for j in range(CONV_WIDTH):
                    start = pl.multiple_of(base + SUBLANES * (N_POS - CONV_WIDTH + 1 + j), SUBLANES)
                    acc = acc + conv_w_ref[j:j + 1, lanes] * a_ext[pl.ds(start, conv_rows), lanes]
                buf0[_rows(i, conv_rows), lanes] = acc
            return c
        lax.fori_loop(0, BLOCK_T // conv_rows, conv_body, 0)

        buf1[...] = _proj(h_p, w_in_ref, _C_Z)

        def ln_body(i, c):
            rows = _rows(i, ROW_TILE)
            cv = buf0[rows, :]
            mu = jnp.mean(cv, axis=-1, keepdims=True)
            cc = cv - mu
            var = jnp.mean(cc * cc, axis=-1, keepdims=True)
            y = cc * lax.rsqrt(var + EPS) * ln_g_ref[...] + ln_b_ref[...]
            u_bf[rows, :] = (_silu(y) * _silu(buf1[rows, :])).astype(BF16)
            return c
        lax.fori_loop(0, n_tiles, ln_body, 0)

        buf0[...] = jnp.dot(u_bf[...], wco_ref[...], preferred_element_type=F32)
        buf1[...] = _proj(h_p, w_in_ref, _C_MC)

        def gate_conv_body(i, c):
            rows = _rows(i, ROW_TILE)
            buf0[rows, :] = _sigmoid(buf1[rows, :]) * buf0[rows, :]
            return c
        lax.fori_loop(0, n_tiles, gate_conv_body, 0)

    l0, l1 = lb_ref[0:1, :], lb_ref[1:2, :]
    lmax = jnp.maximum(l0, l1)
    e0, e1 = jnp.exp(l0 - lmax), jnp.exp(l1 - lmax)
    lb = e0 / (e0 + e1)

    if not is_meta:
        buf1[...] = _proj(h_p, w_in_ref, _C_Q)

        def q_body(i, c):
            rows = _rows(i, ROW_TILE)
            qv = _silu(buf1[rows, :])
            for h in range(HEADS):
                qh[h, rows, :] = qv[:, h * HEAD_DIM:(h + 1) * HEAD_DIM]
            return c
        lax.fori_loop(0, n_tiles, q_body, 0)

    buf1[...] = _proj(h_p, w_in_ref, _C_F)

    def f_body(i, c):
        rows = _rows(i, ROW_TILE)
        fv = lb + (1.0 - lb) * _sigmoid(buf1[rows, :])
        if is_meta:
            row = lax.broadcasted_iota(jnp.int32, fv.shape, 0) + i * ROW_TILE
            token = N_POS * (row % SUBLANES) + row // SUBLANES
            fv = jnp.where(token < BLOCK_T - N_META, 1.0, fv)
        for h in range(HEADS):
            fh[h, rows, :] = fv[:, h * HEAD_DIM:(h + 1) * HEAD_DIM]
        return c
    lax.fori_loop(0, n_tiles, f_body, 0)

    buf1[...] = _proj(h_p, w_in_ref, _C_I)

    def v_body(i, c):
        rows = _rows(i, ROW_TILE)
        vv = buf1[rows, :].astype(BF16)
        for h in range(HEADS):
            vh[h, rows, :] = vv[:, h * HEAD_DIM:(h + 1) * HEAD_DIM]
        return c
    lax.fori_loop(0, n_tiles, v_body, 0)

    nt_dims = (((1,), (1,)), ((), ()))
    tn_dims = (((0,), (0,)), ((), ()))

    def head_body(h, c):
        f_t = [fh[h, a * SUBLANES:(a + 1) * SUBLANES, :] for a in range(N_POS)]
        k_t = [1.0 - f for f in f_t]
        q_t = None if is_meta else [qh[h, a * SUBLANES:(a + 1) * SUBLANES, :] for a in range(N_POS)]
        levels, q_inter, k_state, g_total = _head_levels(q_t, k_t, f_t, want_q=not is_meta)
        v = vh[h]
        upd = lax.dot_general(v, k_state, tn_dims, preferred_element_type=F32)
        if is_meta:
            st[h] = upd
            return c
        s_t = st[h]
        lvl = lvl_ref[...]
        scores = jnp.zeros((BLOCK_T, BLOCK_T), F32)
        for l, (ql, kl) in enumerate(levels):
            part = lax.dot_general(ql, kl, nt_dims, preferred_element_type=F32)
            scores = jnp.where(lvl == l, part, scores)
        o = jnp.dot(scores.astype(BF16), v, preferred_element_type=F32)
        o = o + lax.dot_general(q_inter, s_t.astype(BF16), nt_dims, preferred_element_type=F32)
        oh[h] = o
        st[h] = s_t * g_total + upd
        return c
    lax.fori_loop(0, HEADS, head_body, 0)

    if is_meta:
        return

    buf1[...] = _proj(h_p, w_in_ref, _C_G)

    def gn_body(i, c):
        rows = _rows(i, ROW_TILE)
        for h in range(HEADS):
            lanes = slice(h * HEAD_DIM, (h + 1) * HEAD_DIM)
            o = oh[h, rows, :]
            ms = jnp.mean(o * o, axis=-1, keepdims=True)
            on = o * lax.rsqrt(ms + EPS) * gn_ref[:, lanes]
            u_bf[rows, lanes] = (on * _silu(buf1[rows, lanes])).astype(BF16)
        return c
    lax.fori_loop(0, n_tiles, gn_body, 0)

    buf1[...] = jnp.dot(u_bf[...], wro_ref[...], preferred_element_type=F32)
    buf2[...] = _proj(h_p, w_in_ref, _C_MR)

    def merge_body(i, c):
        rows = _rows(i, ROW_TILE)
        merged = buf0[rows, :] + _sigmoid(buf2[rows, :]) * buf1[rows, :]
        u_bf[rows, :] = merged.astype(BF16)
        return c
    lax.fori_loop(0, n_tiles, merge_body, 0)

    h_nat[...] = jnp.dot(pmat_t_ref[...], u_bf[...], preferred_element_type=F32).astype(BF16)
    buf1[...] = jnp.dot(h_nat[...], wo_ref[...], preferred_element_type=F32)

    def out_body(i, c):
        rows = _rows(i, ROW_TILE)
        y = x_ref[0, rows, :] + buf1[rows, :]
        ms = jnp.mean(y * y, axis=-1, keepdims=True)
        out_ref[0, rows, :] = y * lax.rsqrt(ms + EPS) * fg_ref[...]
        return c
    lax.fori_loop(0, n_tiles, out_body, 0)


def _layout_constants():
    i = np.arange(BLOCK_T)
    token = N_POS * (i % SUBLANES) + i // SUBLANES
    pmat = np.zeros((BLOCK_T, BLOCK_T), np.float32)
    pmat[i, token] = 1.0
    t, s = token[:, None], token[None, :]
    x = np.bitwise_xor(t, s)
    top = np.floor(np.log2(np.maximum(x, 1))).astype(np.int32)
    lvl = np.where(s < t, top, np.where(s == t, DIAG_LEVEL, -1)).astype(np.int32)
    return pmat, lvl


def _const_spec(shape, n_grid):
    zeros = (0,) * len(shape)
    if n_grid == 1:
        return pl.BlockSpec(shape, lambda i: zeros, pipeline_mode=pl.Buffered(1))
    return pl.BlockSpec(shape, lambda b, j: zeros, pipeline_mode=pl.Buffered(1))


def kernel(x, meta_tokens, norm_g, w_in, conv_w, conv_b, ln_g, ln_b, w_conv_out, lb_logits,
           gnorm_g, w_rec_out, w_out, final_g):
    bsz, seq, d = x.shape
    assert d == D_MODEL and seq % BLOCK_T == 0 and N_META <= BLOCK_T
    assert norm_g.shape[0] == 1, "single-layer block"
    n_blocks = seq // BLOCK_T

    pmat_np, lvl_np = _layout_constants()
    pmat = jnp.asarray(pmat_np, BF16)
    pmat_t = jnp.asarray(pmat_np.T, BF16)
    lvl = jnp.asarray(lvl_np)

    row = lambda v: v.reshape(1, D_MODEL).astype(F32)
    w_in_b = w_in[0].astype(BF16)
    wco_b = w_conv_out[0].astype(BF16)
    wro_b = w_rec_out[0].astype(BF16)
    wo_b = w_out[0].astype(BF16)
    norm_g2, conv_b2, ln_g2, ln_b2 = row(norm_g[0]), row(conv_b[0]), row(ln_g[0]), row(ln_b[0])
    gn2, fg2 = row(gnorm_g[0]), row(final_g)
    conv_w2 = conv_w[0].astype(F32)
    lb2 = lb_logits.astype(F32)

    meta_block = jnp.concatenate(
        [jnp.zeros((BLOCK_T - N_META, D_MODEL), x.dtype), meta_tokens.astype(x.dtype)], axis=0)

    big = lambda dt: pltpu.VMEM((BLOCK_T, D_MODEL), dt)
    head_major = lambda dt: pltpu.VMEM((HEADS, BLOCK_T, HEAD_DIM), dt)
    params = lambda sem: pltpu.CompilerParams(dimension_semantics=sem,
                                              vmem_limit_bytes=VMEM_LIMIT_BYTES)

    meta_carry, meta_state = pl.pallas_call(
        functools.partial(_block_kernel, is_meta=True),
        grid=(1,),
        in_specs=[_const_spec((BLOCK_T, D_MODEL), 1), _const_spec(pmat.shape, 1),
                  _const_spec(norm_g2.shape, 1), _const_spec(w_in_b.shape, 1),
                  _const_spec(lb2.shape, 1)],
        out_specs=[pl.BlockSpec((BLOCK_T, D_MODEL), lambda i: (0, 0)),
                   pl.BlockSpec((HEADS, HEAD_DIM, HEAD_DIM), lambda i: (0, 0, 0))],
        out_shape=[jax.ShapeDtypeStruct((BLOCK_T, D_MODEL), F32),
                   jax.ShapeDtypeStruct((HEADS, HEAD_DIM, HEAD_DIM), F32)],
        scratch_shapes=[big(BF16), big(BF16), big(F32), big(F32), head_major(F32), head_major(BF16)],
        compiler_params=params(("arbitrary",)),
        name="meta_block",
    )(meta_block, pmat, norm_g2, w_in_b, lb2)

    consts = [meta_carry, meta_state, pmat, pmat_t, lvl, norm_g2, w_in_b, conv_w2, conv_b2,
              ln_g2, ln_b2, wco_b, lb2, gn2, wro_b, wo_b, fg2]
    out = pl.pallas_call(
        functools.partial(_block_kernel, is_meta=False),
        grid=(bsz, n_blocks),
        in_specs=[pl.BlockSpec((1, BLOCK_T, D_MODEL), lambda b, j: (b, j, 0))]
                 + [_const_spec(c.shape, 2) for c in consts],
        out_specs=pl.BlockSpec((1, BLOCK_T, D_MODEL), lambda b, j: (b, j, 0)),
        out_shape=jax.ShapeDtypeStruct(x.shape, x.dtype),
        scratch_shapes=[big(BF16), big(BF16), big(F32), big(F32), big(F32),
                        pltpu.VMEM((HALO_ROWS + BLOCK_T, D_MODEL), F32), big(F32),
                        pltpu.VMEM((HEADS, HEAD_DIM, HEAD_DIM), F32),
                        head_major(F32), head_major(F32), head_major(BF16), head_major(F32),
                        big(BF16)],
        compiler_params=params(("arbitrary", "arbitrary")),
        name="main_blocks",
    )(x, *consts)
    return out
```

```python
import functools

import numpy as np
import jax
import jax.numpy as jnp
from jax import lax
from jax.experimental import pallas as pl
from jax.experimental.pallas import tpu as pltpu

D_MODEL = 1024
N_META = 16
CONV_WIDTH = 31
HEADS = 8
HEAD_DIM = 128
EPS = 1e-6

SUBLANES = 8
BLOCK_T = 256
N_POS = BLOCK_T // SUBLANES
N_LEVELS = 8
DIAG_LEVEL = N_LEVELS
HALO_ROWS = N_POS * SUBLANES
ROW_TILE = 16
VMEM_LIMIT_BYTES = 56 * 1024 * 1024

F32 = jnp.float32
BF16 = jnp.bfloat16

_C_GLU_A, _C_GLU_B, _C_Z, _C_Q, _C_F, _C_I, _C_G, _C_MC, _C_MR = [i * D_MODEL for i in range(9)]


def _sigmoid(x):
    return jax.nn.sigmoid(x)


def _silu(x):
    return x * jax.nn.sigmoid(x)


def _rows(i, n):
    if isinstance(i, int):
        return slice(i * n, (i + 1) * n)
    return pl.ds(pl.multiple_of(i * n, n), n)


def _tile_loop(n, body):
    for i in range(n):
        body(i, 0)


def _proj(h_ref, w_ref, col):
    return jnp.dot(h_ref[...], w_ref[:, col:col + D_MODEL], preferred_element_type=F32)


def _sub_roll(x, shift):
    return pltpu.roll(x, shift % SUBLANES, 0)


def _cat_bf16(tiles):
    return jnp.concatenate(tiles, axis=0).astype(BF16)


def _head_levels(q_t, k_t, f_t, want_q):
    fwd = list(f_t)
    rev = [None] * N_POS
    levels = []

    def times(x_t, s_t):
        return [x if s is None else x * s for x, s in zip(x_t, s_t)]

    n = 1
    while n < N_POS:
        levels.append((_cat_bf16(times(q_t, fwd)) if want_q else None, _cat_bf16(times(k_t, rev))))
        new_fwd, new_rev = list(fwd), list(rev)
        for b0 in range(0, N_POS, 2 * n):
            tot_lo, tot_up = fwd[b0 + n - 1], fwd[b0 + 2 * n - 1]
            for i in range(n):
                new_fwd[b0 + n + i] = fwd[b0 + n + i] * tot_lo
                new_rev[b0 + i] = tot_up if rev[b0 + i] is None else rev[b0 + i] * tot_up
        fwd, rev = new_fwd, new_rev
        n *= 2

    g = fwd[N_POS - 1]
    sub = lax.broadcasted_iota(jnp.int32, g.shape, 0)
    sub4 = sub % 4
    one = jnp.ones_like(g)
    gm = [None] + [_sub_roll(g, s) for s in range(1, SUBLANES)]
    gp = [None] + [_sub_roll(g, -s) for s in range(1, SUBLANES)]
    u64 = jnp.where(sub4 == 3, gm[1], one)
    w64 = jnp.where(sub4 == 0, gp[1], one)
    u128 = (jnp.where(sub4 >= 1, gm[1], one) * jnp.where(sub4 >= 2, gm[2], one)
            * jnp.where(sub4 >= 3, gm[3], one))
    w128 = (jnp.where(sub4 <= 2, gp[1], one) * jnp.where(sub4 <= 1, gp[2], one)
            * jnp.where(sub4 == 0, gp[3], one))
    px, sx = one, one
    for s in range(1, SUBLANES):
        px = px * jnp.where(sub >= s, gm[s], one)
        sx = sx * jnp.where(sub <= SUBLANES - 1 - s, gp[s], one)
    g_total = (px * g * sx)[0:1, :]

    k_rev = times(k_t, rev)
    q_fwd = times(q_t, fwd) if want_q else None

    def scaled(x_t, s):
        return _cat_bf16([x * s for x in x_t])

    levels.append((_cat_bf16(q_fwd) if want_q else None, _cat_bf16(k_rev)))
    levels.append((scaled(q_fwd, u64) if want_q else None, scaled(k_rev, w64)))
    levels.append((scaled(q_fwd, u128) if want_q else None, scaled(k_rev, w128)))
    levels.append((_cat_bf16(q_t) if want_q else None, _cat_bf16(k_t)))
    q_inter = scaled(q_fwd, px) if want_q else None
    k_state = scaled(k_rev, sx)
    return levels, q_inter, k_state, g_total


def _block_kernel(*refs, is_meta):
    if is_meta:
        (x_ref, pmat_ref, norm_g_ref, w_in_ref, lb_ref,
         carry_out_ref, st_out_ref,
         h_nat, h_p, buf0, buf1, fh, vh) = refs
        carry, st = carry_out_ref, st_out_ref
    else:
        (x_ref, mcarry_ref, mst_ref, pmat_ref, pmat_t_ref, lvl_ref, norm_g_ref, w_in_ref,
         conv_w_ref, conv_b_ref, ln_g_ref, ln_b_ref, wco_ref, lb_ref, gn_ref, wro_ref,
         wo_ref, fg_ref, out_ref,
         h_nat, h_p, buf0, buf1, buf_z, buf_mc, buf_g, buf_mr,
         a_ext, carry, st, qh, fh, vh, oh, u_bf) = refs

        @pl.when(pl.program_id(1) == 0)
        def _():
            carry[...] = mcarry_ref[...]
            st[...] = mst_ref[...]

    n_tiles = BLOCK_T // ROW_TILE

    def x_rows(rows):
        return x_ref[rows, :] if is_meta else x_ref[0, rows, :]

    def norm_body(i, c):
        rows = _rows(i, ROW_TILE)
        xv = x_rows(rows)
        ms = jnp.mean(xv * xv, axis=-1, keepdims=True)
        h_nat[rows, :] = (xv * lax.rsqrt(ms + EPS) * norm_g_ref[...]).astype(BF16)
        return c
    _tile_loop(n_tiles, norm_body)
    h_p[...] = jnp.dot(pmat_ref[...], h_nat[...], preferred_element_type=F32).astype(BF16)

    buf0[...] = _proj(h_p, w_in_ref, _C_GLU_A)
    buf1[...] = _proj(h_p, w_in_ref, _C_GLU_B)

    def glu_body(i, c):
        rows = _rows(i, SUBLANES)
        a = buf0[rows, :] * _sigmoid(buf1[rows, :])
        rolled = pltpu.roll(a, 1, 0)
        if is_meta:
            carry[rows, :] = rolled
        else:
            sub = lax.broadcasted_iota(jnp.int32, a.shape, 0)
            a_ext[_rows(i + HALO_ROWS // SUBLANES, SUBLANES), :] = a
            a_ext[rows, :] = jnp.where(sub == 0, carry[rows, :], rolled)
            carry[rows, :] = rolled
        return c
    _tile_loop(BLOCK_T // SUBLANES, glu_body)

    if not is_meta:
        conv_rows = 32

        def conv_body(i, c):
            base = i * conv_rows
            for l in range(D_MODEL // HEAD_DIM):
                lanes = slice(l * HEAD_DIM, (l + 1) * HEAD_DIM)
                first = base + SUBLANES * (N_POS - CONV_WIDTH + 1)
                span = conv_rows + SUBLANES * (CONV_WIDTH - 1)
                win = a_ext[first:first + span, lanes]
                acc = jnp.broadcast_to(conv_b_ref[:, lanes], (conv_rows, HEAD_DIM))
                for j in range(CONV_WIDTH):
                    acc = acc + conv_w_ref[j:j + 1, lanes] * win[SUBLANES * j:SUBLANES * j + conv_rows]
                buf0[_rows(i, conv_rows), lanes] = acc
            return c

    def conv_tiles(tiles):
        if not is_meta:
            for i in tiles:
                conv_body(i, 0)

    l0, l1 = lb_ref[0:1, :], lb_ref[1:2, :]
    lmax = jnp.maximum(l0, l1)
    e0, e1 = jnp.exp(l0 - lmax), jnp.exp(l1 - lmax)
    lb = e0 / (e0 + e1)

    if not is_meta:
        buf1[...] = _proj(h_p, w_in_ref, _C_Q)
        conv_tiles((0, 1, 2))

        def q_body(i, c):
            rows = _rows(i, ROW_TILE)
            qv = _silu(buf1[rows, :])
            for h in range(HEADS):
                qh[h, rows, :] = qv[:, h * HEAD_DIM:(h + 1) * HEAD_DIM]
            return c
        _tile_loop(n_tiles, q_body)

    buf1[...] = _proj(h_p, w_in_ref, _C_F)
    conv_tiles((3, 4, 5))

    def f_body(i, c):
        rows = _rows(i, ROW_TILE)
        fv = lb + (1.0 - lb) * _sigmoid(buf1[rows, :])
        if is_meta:
            row = lax.broadcasted_iota(jnp.int32, fv.shape, 0) + i * ROW_TILE
            token = N_POS * (row % SUBLANES) + row // SUBLANES
            fv = jnp.where(token < BLOCK_T - N_META, 1.0, fv)
        for h in range(HEADS):
            fh[h, rows, :] = fv[:, h * HEAD_DIM:(h + 1) * HEAD_DIM]
        return c
    _tile_loop(n_tiles, f_body)

    buf1[...] = _proj(h_p, w_in_ref, _C_I)
    conv_tiles((6, 7))

    def v_body(i, c):
        rows = _rows(i, ROW_TILE)
        vv = buf1[rows, :].astype(BF16)
        for h in range(HEADS):
            vh[h, rows, :] = vv[:, h * HEAD_DIM:(h + 1) * HEAD_DIM]
        return c
    _tile_loop(n_tiles, v_body)

    nt_dims = (((1,), (1,)), ((), ()))
    tn_dims = (((0,), (0,)), ((), ()))

    def head_body(h, c):
        f_t = [fh[h, a * SUBLANES:(a + 1) * SUBLANES, :] for a in range(N_POS)]
        k_t = [1.0 - f for f in f_t]
        q_t = None if is_meta else [qh[h, a * SUBLANES:(a + 1) * SUBLANES, :] for a in range(N_POS)]
        levels, q_inter, k_state, g_total = _head_levels(q_t, k_t, f_t, want_q=not is_meta)
        v = vh[h]
        upd = lax.dot_general(v, k_state, tn_dims, preferred_element_type=F32)
        if is_meta:
            st[h] = upd
            return c
        s_t = st[h]
        lvl = lvl_ref[...]
        scores = jnp.zeros((BLOCK_T, BLOCK_T), F32)
        for l, (ql, kl) in enumerate(levels):
            part = lax.dot_general(ql, kl, nt_dims, preferred_element_type=F32)
            scores = jnp.where(lvl == l, part, scores)
        o = jnp.dot(scores.astype(BF16), v, preferred_element_type=F32)
        o = o + lax.dot_general(q_inter, s_t.astype(BF16), nt_dims, preferred_element_type=F32)
        oh[h] = o
        st[h] = s_t * g_total + upd
        return c

    lax.fori_loop(0, HEADS, head_body, 0, unroll=4)

    if is_meta:
        return

    buf_z[...] = _proj(h_p, w_in_ref, _C_Z)
    buf_mc[...] = _proj(h_p, w_in_ref, _C_MC)
    buf_g[...] = _proj(h_p, w_in_ref, _C_G)
    buf_mr[...] = _proj(h_p, w_in_ref, _C_MR)

    def ln_body(i, c):
        rows = _rows(i, ROW_TILE)
        cv = buf0[rows, :]
        mu = jnp.mean(cv, axis=-1, keepdims=True)
        cc = cv - mu
        var = jnp.mean(cc * cc, axis=-1, keepdims=True)
        y = cc * lax.rsqrt(var + EPS) * ln_g_ref[...] + ln_b_ref[...]
        u_bf[rows, :] = (_silu(y) * _silu(buf_z[rows, :])).astype(BF16)
        return c
    _tile_loop(n_tiles, ln_body)

    buf0[...] = jnp.dot(u_bf[...], wco_ref[...], preferred_element_type=F32)

    def gate_conv_body(i, c):
        rows = _rows(i, ROW_TILE)
        buf0[rows, :] = _sigmoid(buf_mc[rows, :]) * buf0[rows, :]
        return c
    _tile_loop(n_tiles, gate_conv_body)


    def gn_body(i, c):
        rows = _rows(i, ROW_TILE)
        for h in range(HEADS):
            lanes = slice(h * HEAD_DIM, (h + 1) * HEAD_DIM)
            o = oh[h, rows, :]
            ms = jnp.mean(o * o, axis=-1, keepdims=True)
            on = o * lax.rsqrt(ms + EPS) * gn_ref[:, lanes]
            u_bf[rows, lanes] = (on * _silu(buf_g[rows, lanes])).astype(BF16)
        return c
    _tile_loop(n_tiles, gn_body)

    buf1[...] = jnp.dot(u_bf[...], wro_ref[...], preferred_element_type=F32)

    def merge_body(i, c):
        rows = _rows(i, ROW_TILE)
        merged = buf0[rows, :] + _sigmoid(buf_mr[rows, :]) * buf1[rows, :]
        u_bf[rows, :] = merged.astype(BF16)
        return c
    _tile_loop(n_tiles, merge_body)

    h_nat[...] = jnp.dot(pmat_t_ref[...], u_bf[...], preferred_element_type=F32).astype(BF16)
    buf1[...] = jnp.dot(h_nat[...], wo_ref[...], preferred_element_type=F32)

    def out_body(i, c):
        rows = _rows(i, ROW_TILE)
        y = x_ref[0, rows, :] + buf1[rows, :]
        ms = jnp.mean(y * y, axis=-1, keepdims=True)
        out_ref[0, rows, :] = y * lax.rsqrt(ms + EPS) * fg_ref[...]
        return c
    _tile_loop(n_tiles, out_body)


def _layout_constants():
    i = np.arange(BLOCK_T)
    token = N_POS * (i % SUBLANES) + i // SUBLANES
    pmat = np.zeros((BLOCK_T, BLOCK_T), np.float32)
    pmat[i, token] = 1.0
    t, s = token[:, None], token[None, :]
    x = np.bitwise_xor(t, s)
    top = np.floor(np.log2(np.maximum(x, 1))).astype(np.int32)
    lvl = np.where(s < t, top, np.where(s == t, DIAG_LEVEL, -1)).astype(np.int32)
    return pmat, lvl


def _const_spec(shape, n_grid):
    zeros = (0,) * len(shape)
    if n_grid == 1:
        return pl.BlockSpec(shape, lambda i: zeros, pipeline_mode=pl.Buffered(1))
    return pl.BlockSpec(shape, lambda b, j: zeros, pipeline_mode=pl.Buffered(1))


def kernel(x, meta_tokens, norm_g, w_in, conv_w, conv_b, ln_g, ln_b, w_conv_out, lb_logits,
           gnorm_g, w_rec_out, w_out, final_g):
    bsz, seq, d = x.shape
    assert d == D_MODEL and seq % BLOCK_T == 0 and N_META <= BLOCK_T
    assert norm_g.shape[0] == 1, "single-layer block"
    n_blocks = seq // BLOCK_T

    pmat_np, lvl_np = _layout_constants()
    pmat = jnp.asarray(pmat_np, BF16)
    pmat_t = jnp.asarray(pmat_np.T, BF16)
    lvl = jnp.asarray(lvl_np)

    row = lambda v: v.reshape(1, D_MODEL).astype(F32)
    w_in_b = w_in[0].astype(BF16)
    wco_b = w_conv_out[0].astype(BF16)
    wro_b = w_rec_out[0].astype(BF16)
    wo_b = w_out[0].astype(BF16)
    norm_g2, conv_b2, ln_g2, ln_b2 = row(norm_g[0]), row(conv_b[0]), row(ln_g[0]), row(ln_b[0])
    gn2, fg2 = row(gnorm_g[0]), row(final_g)
    conv_w2 = conv_w[0].astype(F32)
    lb2 = lb_logits.astype(F32)

    meta_block = jnp.concatenate(
        [jnp.zeros((BLOCK_T - N_META, D_MODEL), x.dtype), meta_tokens.astype(x.dtype)], axis=0)

    big = lambda dt: pltpu.VMEM((BLOCK_T, D_MODEL), dt)
    head_major = lambda dt: pltpu.VMEM((HEADS, BLOCK_T, HEAD_DIM), dt)
    params = lambda sem: pltpu.CompilerParams(dimension_semantics=sem,
                                              vmem_limit_bytes=VMEM_LIMIT_BYTES)

    meta_carry, meta_state = pl.pallas_call(
        functools.partial(_block_kernel, is_meta=True),
        grid=(1,),
        in_specs=[_const_spec((BLOCK_T, D_MODEL), 1), _const_spec(pmat.shape, 1),
                  _const_spec(norm_g2.shape, 1), _const_spec(w_in_b.shape, 1),
                  _const_spec(lb2.shape, 1)],
        out_specs=[pl.BlockSpec((BLOCK_T, D_MODEL), lambda i: (0, 0)),
                   pl.BlockSpec((HEADS, HEAD_DIM, HEAD_DIM), lambda i: (0, 0, 0))],
        out_shape=[jax.ShapeDtypeStruct((BLOCK_T, D_MODEL), F32),
                   jax.ShapeDtypeStruct((HEADS, HEAD_DIM, HEAD_DIM), F32)],
        scratch_shapes=[big(BF16), big(BF16), big(F32), big(F32), head_major(F32), head_major(BF16)],
        compiler_params=params(("arbitrary",)),
        name="meta_block",
    )(meta_block, pmat, norm_g2, w_in_b, lb2)

    consts = [meta_carry, meta_state, pmat, pmat_t, lvl, norm_g2, w_in_b, conv_w2, conv_b2,
              ln_g2, ln_b2, wco_b, lb2, gn2, wro_b, wo_b, fg2]
    out = pl.pallas_call(
        functools.partial(_block_kernel, is_meta=False),
        grid=(bsz, n_blocks),
        in_specs=[pl.BlockSpec((1, BLOCK_T, D_MODEL), lambda b, j: (b, j, 0))]
                 + [_const_spec(c.shape, 2) for c in consts],
        out_specs=pl.BlockSpec((1, BLOCK_T, D_MODEL), lambda b, j: (b, j, 0)),
        out_shape=jax.ShapeDtypeStruct(x.shape, x.dtype),
        scratch_shapes=[big(BF16), big(BF16)] + [big(F32)] * 6 + [
                        pltpu.VMEM((HALO_ROWS + BLOCK_T, D_MODEL), F32), big(F32),
                        pltpu.VMEM((HEADS, HEAD_DIM, HEAD_DIM), F32),
                        head_major(F32), head_major(F32), head_major(BF16), head_major(F32),
                        big(BF16)],
        compiler_params=params(("arbitrary", "arbitrary")),
        name="main_blocks",
    )(x, *consts)
    return out
```

```python
import functools

import numpy as np
import jax
import jax.numpy as jnp
from jax import lax
from jax.experimental import pallas as pl
from jax.experimental.pallas import tpu as pltpu

D_MODEL = 1024
N_META = 16
CONV_WIDTH = 31
HEADS = 8
HEAD_DIM = 128
EPS = 1e-6

SUBLANES = 8
BLOCK_T = 256
N_POS = BLOCK_T // SUBLANES
N_LEVELS = 8
DIAG_LEVEL = N_LEVELS
HALO_ROWS = N_POS * SUBLANES
ROW_TILE = 16
HEAD_UNROLL = 8
VMEM_LIMIT_BYTES = 56 * 1024 * 1024

F32 = jnp.float32
BF16 = jnp.bfloat16

_C_GLU_A, _C_GLU_B, _C_Z, _C_Q, _C_F, _C_I, _C_G, _C_MC, _C_MR = [i * D_MODEL for i in range(9)]


def _sigmoid(x):
    return jax.nn.sigmoid(x)


def _silu(x):
    return x * jax.nn.sigmoid(x)


def _rows(i, n):
    if isinstance(i, int):
        return slice(i * n, (i + 1) * n)
    return pl.ds(pl.multiple_of(i * n, n), n)


def _tile_loop(n, body):
    for i in range(n):
        body(i, 0)


def _proj(h_ref, w_ref, col):
    return jnp.dot(h_ref[...], w_ref[:, col:col + D_MODEL], preferred_element_type=F32)


def _sub_roll(x, shift):
    return pltpu.roll(x, shift % SUBLANES, 0)


def _cat_bf16(tiles):
    return jnp.concatenate(tiles, axis=0).astype(BF16)


def _head_levels(q_t, k_t, f_t, want_q):
    fwd = list(f_t)
    rev = [None] * N_POS
    levels = []

    def times(x_t, s_t):
        return [x if s is None else x * s for x, s in zip(x_t, s_t)]

    n = 1
    while n < N_POS:
        levels.append((_cat_bf16(times(q_t, fwd)) if want_q else None, _cat_bf16(times(k_t, rev))))
        new_fwd, new_rev = list(fwd), list(rev)
        for b0 in range(0, N_POS, 2 * n):
            tot_lo, tot_up = fwd[b0 + n - 1], fwd[b0 + 2 * n - 1]
            for i in range(n):
                new_fwd[b0 + n + i] = fwd[b0 + n + i] * tot_lo
                new_rev[b0 + i] = tot_up if rev[b0 + i] is None else rev[b0 + i] * tot_up
        fwd, rev = new_fwd, new_rev
        n *= 2

    g = fwd[N_POS - 1]
    sub = lax.broadcasted_iota(jnp.int32, g.shape, 0)
    sub4 = sub % 4
    one = jnp.ones_like(g)
    gm = [None] + [_sub_roll(g, s) for s in range(1, SUBLANES)]
    gp = [None] + [_sub_roll(g, -s) for s in range(1, SUBLANES)]
    u64 = jnp.where(sub4 == 3, gm[1], one)
    w64 = jnp.where(sub4 == 0, gp[1], one)
    u128 = (jnp.where(sub4 >= 1, gm[1], one) * jnp.where(sub4 >= 2, gm[2], one)
            * jnp.where(sub4 >= 3, gm[3], one))
    w128 = (jnp.where(sub4 <= 2, gp[1], one) * jnp.where(sub4 <= 1, gp[2], one)
            * jnp.where(sub4 == 0, gp[3], one))
    px, sx = one, one
    for s in range(1, SUBLANES):
        px = px * jnp.where(sub >= s, gm[s], one)
        sx = sx * jnp.where(sub <= SUBLANES - 1 - s, gp[s], one)
    g_total = (px * g * sx)[0:1, :]

    k_rev = times(k_t, rev)
    q_fwd = times(q_t, fwd) if want_q else None

    def scaled(x_t, s):
        return _cat_bf16([x * s for x in x_t])

    levels.append((_cat_bf16(q_fwd) if want_q else None, _cat_bf16(k_rev)))
    levels.append((scaled(q_fwd, u64) if want_q else None, scaled(k_rev, w64)))
    levels.append((scaled(q_fwd, u128) if want_q else None, scaled(k_rev, w128)))
    levels.append((_cat_bf16(q_t) if want_q else None, _cat_bf16(k_t)))
    q_inter = scaled(q_fwd, px) if want_q else None
    k_state = scaled(k_rev, sx)
    return levels, q_inter, k_state, g_total


def _block_kernel(*refs, is_meta):
    if is_meta:
        (x_ref, pmat_ref, norm_g_ref, w_in_ref, lb_ref,
         carry_out_ref, st_out_ref,
         h_nat, h_p, buf0, buf1, fh, vh) = refs
        carry, st = carry_out_ref, st_out_ref
    else:
        (x_ref, mcarry_ref, mst_ref, pmat_ref, pmat_t_ref, lvl_ref, norm_g_ref, w_in_ref,
         conv_w_ref, conv_b_ref, ln_g_ref, ln_b_ref, wco_ref, lb_ref, gn_ref, wro_ref,
         wo_ref, fg_ref, out_ref,
         h_nat, h_p, buf0, buf1, buf_z, buf_mc, buf_g, buf_mr,
         a_ext, carry, st, qh, fh, vh, oh, u_bf) = refs

        @pl.when(pl.program_id(1) == 0)
        def _():
            carry[...] = mcarry_ref[...]
            st[...] = mst_ref[...]

    n_tiles = BLOCK_T // ROW_TILE

    def x_rows(rows):
        return x_ref[rows, :] if is_meta else x_ref[0, rows, :]

    def norm_body(i, c):
        rows = _rows(i, ROW_TILE)
        xv = x_rows(rows)
        ms = jnp.mean(xv * xv, axis=-1, keepdims=True)
        h_nat[rows, :] = (xv * lax.rsqrt(ms + EPS) * norm_g_ref[...]).astype(BF16)
        return c
    _tile_loop(n_tiles, norm_body)
    h_p[...] = jnp.dot(pmat_ref[...], h_nat[...], preferred_element_type=F32).astype(BF16)

    buf0[...] = _proj(h_p, w_in_ref, _C_GLU_A)
    buf1[...] = _proj(h_p, w_in_ref, _C_GLU_B)

    def glu_body(i, c):
        rows = _rows(i, SUBLANES)
        a = buf0[rows, :] * _sigmoid(buf1[rows, :])
        rolled = pltpu.roll(a, 1, 0)
        if is_meta:
            carry[rows, :] = rolled
        else:
            sub = lax.broadcasted_iota(jnp.int32, a.shape, 0)
            a_ext[_rows(i + HALO_ROWS // SUBLANES, SUBLANES), :] = a
            a_ext[rows, :] = jnp.where(sub == 0, carry[rows, :], rolled)
            carry[rows, :] = rolled
        return c
    _tile_loop(BLOCK_T // SUBLANES, glu_body)

    if not is_meta:
        conv_rows = 32

        def conv_body(i, c):
            base = i * conv_rows
            for l in range(D_MODEL // HEAD_DIM):
                lanes = slice(l * HEAD_DIM, (l + 1) * HEAD_DIM)
                first = base + SUBLANES * (N_POS - CONV_WIDTH + 1)
                span = conv_rows + SUBLANES * (CONV_WIDTH - 1)
                win = a_ext[first:first + span, lanes]
                acc = jnp.broadcast_to(conv_b_ref[:, lanes], (conv_rows, HEAD_DIM))
                for j in range(CONV_WIDTH):
                    acc = acc + conv_w_ref[j:j + 1, lanes] * win[SUBLANES * j:SUBLANES * j + conv_rows]
                buf0[_rows(i, conv_rows), lanes] = acc
            return c
        _tile_loop(BLOCK_T // conv_rows, conv_body)

    l0, l1 = lb_ref[0:1, :], lb_ref[1:2, :]
    lmax = jnp.maximum(l0, l1)
    e0, e1 = jnp.exp(l0 - lmax), jnp.exp(l1 - lmax)
    lb = e0 / (e0 + e1)

    if not is_meta:
        buf1[...] = _proj(h_p, w_in_ref, _C_Q)

        def q_body(i, c):
            rows = _rows(i, ROW_TILE)
            qv = _silu(buf1[rows, :])
            for h in range(HEADS):
                qh[h, rows, :] = qv[:, h * HEAD_DIM:(h + 1) * HEAD_DIM]
            return c
        _tile_loop(n_tiles, q_body)

    buf1[...] = _proj(h_p, w_in_ref, _C_F)

    def f_body(i, c):
        rows = _rows(i, ROW_TILE)
        fv = lb + (1.0 - lb) * _sigmoid(buf1[rows, :])
        if is_meta:
            row = lax.broadcasted_iota(jnp.int32, fv.shape, 0) + i * ROW_TILE
            token = N_POS * (row % SUBLANES) + row // SUBLANES
            fv = jnp.where(token < BLOCK_T - N_META, 1.0, fv)
        for h in range(HEADS):
            fh[h, rows, :] = fv[:, h * HEAD_DIM:(h + 1) * HEAD_DIM]
        return c
    _tile_loop(n_tiles, f_body)

    buf1[...] = _proj(h_p, w_in_ref, _C_I)

    def v_body(i, c):
        rows = _rows(i, ROW_TILE)
        vv = buf1[rows, :].astype(BF16)
        for h in range(HEADS):
            vh[h, rows, :] = vv[:, h * HEAD_DIM:(h + 1) * HEAD_DIM]
        return c
    _tile_loop(n_tiles, v_body)

    nt_dims = (((1,), (1,)), ((), ()))
    tn_dims = (((0,), (0,)), ((), ()))

    def head_body(h, c):
        f_t = [fh[h, a * SUBLANES:(a + 1) * SUBLANES, :] for a in range(N_POS)]
        k_t = [1.0 - f for f in f_t]
        q_t = None if is_meta else [qh[h, a * SUBLANES:(a + 1) * SUBLANES, :] for a in range(N_POS)]
        levels, q_inter, k_state, g_total = _head_levels(q_t, k_t, f_t, want_q=not is_meta)
        v = vh[h]
        upd = lax.dot_general(v, k_state, tn_dims, preferred_element_type=F32)
        if is_meta:
            st[h] = upd
            return c
        s_t = st[h]
        lvl = lvl_ref[...]
        scores = jnp.zeros((BLOCK_T, BLOCK_T), BF16)
        for l, (ql, kl) in enumerate(levels):
            part = lax.dot_general(ql, kl, nt_dims, preferred_element_type=F32)
            scores = jnp.where(lvl == l, part.astype(BF16), scores)
        o = jnp.dot(scores, v, preferred_element_type=F32)
        o = o + lax.dot_general(q_inter, s_t.astype(BF16), nt_dims, preferred_element_type=F32)
        oh[h] = o
        st[h] = s_t * g_total + upd
        return c

    lax.fori_loop(0, HEADS, head_body, 0, unroll=HEAD_UNROLL)

    if is_meta:
        return

    buf_z[...] = _proj(h_p, w_in_ref, _C_Z)
    buf_mc[...] = _proj(h_p, w_in_ref, _C_MC)
    buf_g[...] = _proj(h_p, w_in_ref, _C_G)
    buf_mr[...] = _proj(h_p, w_in_ref, _C_MR)

    def ln_body(i, c):
        rows = _rows(i, ROW_TILE)
        cv = buf0[rows, :]
        mu = jnp.mean(cv, axis=-1, keepdims=True)
        cc = cv - mu
        var = jnp.mean(cc * cc, axis=-1, keepdims=True)
        y = cc * lax.rsqrt(var + EPS) * ln_g_ref[...] + ln_b_ref[...]
        u_bf[rows, :] = (_silu(y) * _silu(buf_z[rows, :])).astype(BF16)
        return c
    _tile_loop(n_tiles, ln_body)

    buf0[...] = jnp.dot(u_bf[...], wco_ref[...], preferred_element_type=F32)

    def gate_conv_body(i, c):
        rows = _rows(i, ROW_TILE)
        buf0[rows, :] = _sigmoid(buf_mc[rows, :]) * buf0[rows, :]
        return c
    _tile_loop(n_tiles, gate_conv_body)

    def gn_body(i, c):
        rows = _rows(i, ROW_TILE)
        for h in range(HEADS):
            lanes = slice(h * HEAD_DIM, (h + 1) * HEAD_DIM)
            o = oh[h, rows, :]
            ms = jnp.mean(o * o, axis=-1, keepdims=True)
            on = o * lax.rsqrt(ms + EPS) * gn_ref[:, lanes]
            u_bf[rows, lanes] = (on * _silu(buf_g[rows, lanes])).astype(BF16)
        return c
    _tile_loop(n_tiles, gn_body)

    buf1[...] = jnp.dot(u_bf[...], wro_ref[...], preferred_element_type=F32)

    def merge_body(i, c):
        rows = _rows(i, ROW_TILE)
        merged = buf0[rows, :] + _sigmoid(buf_mr[rows, :]) * buf1[rows, :]
        u_bf[rows, :] = merged.astype(BF16)
        return c
    _tile_loop(n_tiles, merge_body)

    h_nat[...] = jnp.dot(pmat_t_ref[...], u_bf[...], preferred_element_type=F32).astype(BF16)
    buf1[...] = jnp.dot(h_nat[...], wo_ref[...], preferred_element_type=F32)

    def out_body(i, c):
        rows = _rows(i, ROW_TILE)
        y = x_ref[0, rows, :] + buf1[rows, :]
        ms = jnp.mean(y * y, axis=-1, keepdims=True)
        out_ref[0, rows, :] = y * lax.rsqrt(ms + EPS) * fg_ref[...]
        return c
    _tile_loop(n_tiles, out_body)


def _layout_constants():
    i = np.arange(BLOCK_T)
    token = N_POS * (i % SUBLANES) + i // SUBLANES
    pmat = np.zeros((BLOCK_T, BLOCK_T), np.float32)
    pmat[i, token] = 1.0
    t, s = token[:, None], token[None, :]
    x = np.bitwise_xor(t, s)
    top = np.floor(np.log2(np.maximum(x, 1))).astype(np.int32)
    lvl = np.where(s < t, top, np.where(s == t, DIAG_LEVEL, -1)).astype(np.int32)
    return pmat, lvl


def _const_spec(shape, n_grid):
    zeros = (0,) * len(shape)
    if n_grid == 1:
        return pl.BlockSpec(shape, lambda i: zeros, pipeline_mode=pl.Buffered(1))
    return pl.BlockSpec(shape, lambda b, j: zeros, pipeline_mode=pl.Buffered(1))


def kernel(x, meta_tokens, norm_g, w_in, conv_w, conv_b, ln_g, ln_b, w_conv_out, lb_logits,
           gnorm_g, w_rec_out, w_out, final_g):
    bsz, seq, d = x.shape
    assert d == D_MODEL and seq % BLOCK_T == 0 and N_META <= BLOCK_T
    assert norm_g.shape[0] == 1, "single-layer block"
    n_blocks = seq // BLOCK_T

    pmat_np, lvl_np = _layout_constants()
    pmat = jnp.asarray(pmat_np, BF16)
    pmat_t = jnp.asarray(pmat_np.T, BF16)
    lvl = jnp.asarray(lvl_np, BF16)

    row = lambda v: v.reshape(1, D_MODEL).astype(F32)
    w_in_b = w_in[0].astype(BF16)
    wco_b = w_conv_out[0].astype(BF16)
    wro_b = w_rec_out[0].astype(BF16)
    wo_b = w_out[0].astype(BF16)
    norm_g2, conv_b2, ln_g2, ln_b2 = row(norm_g[0]), row(conv_b[0]), row(ln_g[0]), row(ln_b[0])
    gn2, fg2 = row(gnorm_g[0]), row(final_g)
    conv_w2 = conv_w[0].astype(F32)
    lb2 = lb_logits.astype(F32)

    meta_block = jnp.concatenate(
        [jnp.zeros((BLOCK_T - N_META, D_MODEL), x.dtype), meta_tokens.astype(x.dtype)], axis=0)

    big = lambda dt: pltpu.VMEM((BLOCK_T, D_MODEL), dt)
    head_major = lambda dt: pltpu.VMEM((HEADS, BLOCK_T, HEAD_DIM), dt)
    params = lambda sem: pltpu.CompilerParams(dimension_semantics=sem,
                                              vmem_limit_bytes=VMEM_LIMIT_BYTES)

    meta_carry, meta_state = pl.pallas_call(
        functools.partial(_block_kernel, is_meta=True),
        grid=(1,),
        in_specs=[_const_spec((BLOCK_T, D_MODEL), 1), _const_spec(pmat.shape, 1),
                  _const_spec(norm_g2.shape, 1), _const_spec(w_in_b.shape, 1),
                  _const_spec(lb2.shape, 1)],
        out_specs=[pl.BlockSpec((BLOCK_T, D_MODEL), lambda i: (0, 0)),
                   pl.BlockSpec((HEADS, HEAD_DIM, HEAD_DIM), lambda i: (0, 0, 0))],
        out_shape=[jax.ShapeDtypeStruct((BLOCK_T, D_MODEL), F32),
                   jax.ShapeDtypeStruct((HEADS, HEAD_DIM, HEAD_DIM), F32)],
        scratch_shapes=[big(BF16), big(BF16), big(F32), big(F32), head_major(F32), head_major(BF16)],
        compiler_params=params(("arbitrary",)),
        name="meta_block",
    )(meta_block, pmat, norm_g2, w_in_b, lb2)

    consts = [meta_carry, meta_state, pmat, pmat_t, lvl, norm_g2, w_in_b, conv_w2, conv_b2,
              ln_g2, ln_b2, wco_b, lb2, gn2, wro_b, wo_b, fg2]
    out = pl.pallas_call(
        functools.partial(_block_kernel, is_meta=False),
        grid=(bsz, n_blocks),
        in_specs=[pl.BlockSpec((1, BLOCK_T, D_MODEL), lambda b, j: (b, j, 0))]
                 + [_const_spec(c.shape, 2) for c in consts],
        out_specs=pl.BlockSpec((1, BLOCK_T, D_MODEL), lambda b, j: (b, j, 0)),
        out_shape=jax.ShapeDtypeStruct(x.shape, x.dtype),
        scratch_shapes=[big(BF16), big(BF16)] + [big(F32)] * 6 + [
                        pltpu.VMEM((HALO_ROWS + BLOCK_T, D_MODEL), F32), big(F32),
                        pltpu.VMEM((HEADS, HEAD_DIM, HEAD_DIM), F32),
                        head_major(F32), head_major(F32), head_major(BF16), head_major(F32),
                        big(BF16)],
        compiler_params=params(("arbitrary", "arbitrary")),
        name="main_blocks",
    )(x, *consts)
    return out
```

```python
import numpy as np
import jax
import jax.numpy as jnp
from jax import lax
from jax.experimental import pallas as pl
from jax.experimental.pallas import tpu as pltpu

D_MODEL = 1024
N_META = 16
CONV_WIDTH = 31
HEADS = 8
HEAD_DIM = 128
EPS = 1e-6

SUBLANES = 8
BLOCK_T = 256
N_POS = BLOCK_T // SUBLANES
N_LEVELS = 8
DIAG_LEVEL = N_LEVELS
HALO_ROWS = N_POS * SUBLANES
ROW_TILE = 16
PACKED_ROWS = 2 * SUBLANES
CONV_POSITIONS = 4
CONV_TAP_GROUP = 4
VMEM_LIMIT_BYTES = 56 * 1024 * 1024
assert N_POS >= CONV_WIDTH - 1

F32 = jnp.float32
BF16 = jnp.bfloat16
NT_DIMS = (((1,), (1,)), ((), ()))
TN_DIMS = (((0,), (0,)), ((), ()))

_C_GLU_A, _C_GLU_B, _C_Z, _C_Q, _C_F, _C_I, _C_G, _C_MC, _C_MR = [i * D_MODEL for i in range(9)]


def _sigmoid(x):
    return jax.nn.sigmoid(x)


def _silu(x):
    return x * jax.nn.sigmoid(x)


def _rows(i, n):
    return slice(i * n, (i + 1) * n)


def _tile_loop(n, body):
    for i in range(n):
        body(i)


def _proj(h_ref, w_ref, col):
    return jnp.dot(h_ref[...], w_ref[:, col:col + D_MODEL], preferred_element_type=F32)


def _sub_roll(x, shift):
    return pltpu.roll(x, shift % SUBLANES, 0)


def _cat_bf16(tiles):
    return jnp.concatenate(tiles, axis=0).astype(BF16)


def _head_levels(q_t, k_t, f_t, want_q):
    fwd = list(f_t)
    rev = [None] * N_POS
    levels = []

    def times(x_t, s_t):
        return [x if s is None else x * s for x, s in zip(x_t, s_t)]

    n = 1
    while n < N_POS:
        levels.append((_cat_bf16(times(q_t, fwd)) if want_q else None, _cat_bf16(times(k_t, rev))))
        new_fwd, new_rev = list(fwd), list(rev)
        for b0 in range(0, N_POS, 2 * n):
            tot_lo, tot_up = fwd[b0 + n - 1], fwd[b0 + 2 * n - 1]
            for i in range(n):
                new_fwd[b0 + n + i] = fwd[b0 + n + i] * tot_lo
                new_rev[b0 + i] = tot_up if rev[b0 + i] is None else rev[b0 + i] * tot_up
        fwd, rev = new_fwd, new_rev
        n *= 2

    g = fwd[N_POS - 1]
    sub = lax.broadcasted_iota(jnp.int32, g.shape, 0)
    sub4 = sub % 4
    one = jnp.ones_like(g)
    gm = [None] + [_sub_roll(g, s) for s in range(1, SUBLANES)]
    gp = [None] + [_sub_roll(g, -s) for s in range(1, SUBLANES)]
    u64 = jnp.where(sub4 == 3, gm[1], one)
    w64 = jnp.where(sub4 == 0, gp[1], one)
    u128 = (jnp.where(sub4 >= 1, gm[1], one) * jnp.where(sub4 >= 2, gm[2], one)
            * jnp.where(sub4 >= 3, gm[3], one))
    w128 = (jnp.where(sub4 <= 2, gp[1], one) * jnp.where(sub4 <= 1, gp[2], one)
            * jnp.where(sub4 == 0, gp[3], one))
    px, sx = one, one
    for s in range(1, SUBLANES):
        px = px * jnp.where(sub >= s, gm[s], one)
        sx = sx * jnp.where(sub <= SUBLANES - 1 - s, gp[s], one)
    g_total = (px * g * sx)[0:1, :]

    k_rev = times(k_t, rev)
    q_fwd = times(q_t, fwd) if want_q else None

    def scaled(x_t, s):
        return _cat_bf16([x * s for x in x_t])

    levels.append((_cat_bf16(q_fwd) if want_q else None, _cat_bf16(k_rev)))
    levels.append((scaled(q_fwd, u64) if want_q else None, scaled(k_rev, w64)))
    levels.append((scaled(q_fwd, u128) if want_q else None, scaled(k_rev, w128)))
    levels.append((_cat_bf16(q_t) if want_q else None, _cat_bf16(k_t)))
    q_inter = scaled(q_fwd, px) if want_q else None
    k_state = scaled(k_rev, sx)
    return levels, q_inter, k_state, g_total


def _norm_and_permute(x_rows, norm_g_ref, pmat_ref, h_nat, h_p):
    def body(i):
        rows = _rows(i, ROW_TILE)
        xv = x_rows(rows)
        ms = jnp.mean(xv * xv, axis=-1, keepdims=True)
        h_nat[rows, :] = (xv * lax.rsqrt(ms + EPS) * norm_g_ref[...]).astype(BF16)
    _tile_loop(BLOCK_T // ROW_TILE, body)
    h_p[...] = jnp.dot(pmat_ref[...], h_nat[...], preferred_element_type=F32).astype(BF16)


def _glu_and_history(glu_a, glu_b, carry, a_ext):
    def body(i):
        rows = _rows(i, SUBLANES)
        a = glu_a[rows, :] * _sigmoid(glu_b[rows, :])
        rolled = pltpu.roll(a, 1, 0)
        if a_ext is not None:
            sub = lax.broadcasted_iota(jnp.int32, a.shape, 0)
            a_ext[_rows(i + N_POS, PACKED_ROWS), :] = _pack_halves(a)
            a_ext[_rows(i, PACKED_ROWS), :] = _pack_halves(
                jnp.where(sub == 0, carry[rows, :], rolled))
        carry[rows, :] = rolled
    _tile_loop(N_POS, body)


def _pack_halves(x):
    half = x.shape[1] // 2
    return jnp.concatenate([x[:, :half], x[:, half:]], axis=0).astype(BF16)


def _causal_conv(a_ext, conv_w_ref, conv_b_ref, out):
    half = D_MODEL // 2
    rows_out = PACKED_ROWS * CONV_POSITIONS
    for p0 in range(0, N_POS, CONV_POSITIONS):
        first = PACKED_ROWS * (p0 + N_POS - CONV_WIDTH + 1)
        for l in range(half // HEAD_DIM):
            lanes = slice(l * HEAD_DIM, (l + 1) * HEAD_DIM)
            win = a_ext[first:first + PACKED_ROWS * (CONV_POSITIONS + CONV_WIDTH - 1), lanes]
            acc = jnp.concatenate([conv_b_ref[:, lanes]] * CONV_POSITIONS, axis=0)
            for j0 in range(0, CONV_WIDTH, CONV_TAP_GROUP):
                part = None
                for j in range(j0, min(j0 + CONV_TAP_GROUP, CONV_WIDTH)):
                    wj = conv_w_ref[PACKED_ROWS * j:PACKED_ROWS * (j + 1), lanes]
                    term = win[PACKED_ROWS * j:PACKED_ROWS * j + rows_out] * jnp.concatenate(
                        [wj] * CONV_POSITIONS, axis=0)
                    part = term if part is None else part + term
                acc = acc + part.astype(F32)
            for q in range(CONV_POSITIONS):
                rows = _rows(p0 + q, SUBLANES)
                out[rows, lanes] = acc[PACKED_ROWS * q:PACKED_ROWS * q + SUBLANES]
                out[rows, half + l * HEAD_DIM:half + (l + 1) * HEAD_DIM] = (
                    acc[PACKED_ROWS * q + SUBLANES:PACKED_ROWS * (q + 1)])


def _recurrence_inputs(h_p, w_in_ref, lb_ref, buf, qh, fh, vh, pad_front):
    l0, l1 = lb_ref[0:1, :], lb_ref[1:2, :]
    lmax = jnp.maximum(l0, l1)
    e0, e1 = jnp.exp(l0 - lmax), jnp.exp(l1 - lmax)
    lb = e0 / (e0 + e1)
    n_tiles = BLOCK_T // ROW_TILE

    def to_heads(dst, val, rows):
        for h in range(HEADS):
            dst[h, rows, :] = val[:, h * HEAD_DIM:(h + 1) * HEAD_DIM]

    if qh is not None:
        buf[...] = _proj(h_p, w_in_ref, _C_Q)
        _tile_loop(n_tiles, lambda i: to_heads(qh, _silu(buf[_rows(i, ROW_TILE), :]),
                                               _rows(i, ROW_TILE)))

    buf[...] = _proj(h_p, w_in_ref, _C_F)

    def f_body(i):
        rows = _rows(i, ROW_TILE)
        fv = lb + (1.0 - lb) * _sigmoid(buf[rows, :])
        if pad_front:
            row = lax.broadcasted_iota(jnp.int32, fv.shape, 0) + i * ROW_TILE
            token = N_POS * (row % SUBLANES) + row // SUBLANES
            fv = jnp.where(token < BLOCK_T - N_META, 1.0, fv)
        to_heads(fh, fv, rows)
    _tile_loop(n_tiles, f_body)

    buf[...] = _proj(h_p, w_in_ref, _C_I)
    _tile_loop(n_tiles, lambda i: to_heads(vh, buf[_rows(i, ROW_TILE), :].astype(BF16),
                                           _rows(i, ROW_TILE)))


def _recurrence_head(h, qh, fh, vh, st, lvl_ref, oh):
    want_q = qh is not None
    f_t = [fh[h, _rows(a, SUBLANES), :] for a in range(N_POS)]
    k_t = [1.0 - f for f in f_t]
    q_t = [qh[h, _rows(a, SUBLANES), :] for a in range(N_POS)] if want_q else None
    levels, q_inter, k_state, g_total = _head_levels(q_t, k_t, f_t, want_q)
    v = vh[h]
    upd = lax.dot_general(v, k_state, TN_DIMS, preferred_element_type=F32)
    if not want_q:
        st[h] = upd
        return
    s_t = st[h]
    lvl = lvl_ref[...]
    scores = jnp.zeros((BLOCK_T, BLOCK_T), BF16)
    for l, (ql, kl) in enumerate(levels):
        part = lax.dot_general(ql, kl, NT_DIMS, preferred_element_type=F32)
        scores = jnp.where(lvl == l, part.astype(BF16), scores)
    o = jnp.dot(scores, v, preferred_element_type=F32)
    o = o + lax.dot_general(q_inter, s_t.astype(BF16), NT_DIMS, preferred_element_type=F32)
    oh[h] = o
    st[h] = s_t * g_total + upd


def _meta_kernel(x_ref, pmat_ref, norm_g_ref, w_in_ref, lb_ref, carry_out, st_out,
                 h_nat, h_p, buf0, buf1, fh, vh):
    _norm_and_permute(lambda rows: x_ref[rows, :], norm_g_ref, pmat_ref, h_nat, h_p)
    buf0[...] = _proj(h_p, w_in_ref, _C_GLU_A)
    buf1[...] = _proj(h_p, w_in_ref, _C_GLU_B)
    _glu_and_history(buf0, buf1, carry_out, None)
    _recurrence_inputs(h_p, w_in_ref, lb_ref, buf1, None, fh, vh, pad_front=True)
    for h in range(HEADS):
        _recurrence_head(h, None, fh, vh, st_out, None, None)


def _main_kernel(x_ref, mcarry_ref, mst_ref, pmat_ref, pmat_t_ref, lvl_ref, norm_g_ref,
                 w_in_ref, conv_w_ref, conv_b_ref, ln_g_ref, ln_b_ref, wco_ref, lb_ref, gn_ref,
                 wro_ref, wo_ref, fg_ref, out_ref,
                 h_nat, h_p, conv_out, qh, fh, vh,
                 buf0, buf1, buf2, buf3, buf_z, buf_mc, buf_g, buf_mr,
                 a_ext, carry, st, oh, u_bf):
    @pl.when(pl.program_id(1) == 0)
    def _():
        carry[...] = mcarry_ref[...]
        st[...] = mst_ref[...]

    n_tiles = BLOCK_T // ROW_TILE

    _norm_and_permute(lambda rows: x_ref[0, rows, :], norm_g_ref, pmat_ref, h_nat, h_p)
    buf0[...] = _proj(h_p, w_in_ref, _C_GLU_A)
    buf1[...] = _proj(h_p, w_in_ref, _C_GLU_B)
    _glu_and_history(buf0, buf1, carry, a_ext)
    _causal_conv(a_ext, conv_w_ref, conv_b_ref, conv_out)
    _recurrence_inputs(h_p, w_in_ref, lb_ref, buf1, qh, fh, vh, pad_front=False)

    buf_z[...] = _proj(h_p, w_in_ref, _C_Z)
    buf_mc[...] = _proj(h_p, w_in_ref, _C_MC)
    buf_g[...] = _proj(h_p, w_in_ref, _C_G)
    buf_mr[...] = _proj(h_p, w_in_ref, _C_MR)

    for h in range(HEADS):
        _recurrence_head(h, qh, fh, vh, st, lvl_ref, oh)

    def ln_body(i):
        rows = _rows(i, ROW_TILE)
        cv = conv_out[rows, :]
        mu = jnp.mean(cv, axis=-1, keepdims=True)
        cc = cv - mu
        var = jnp.mean(cc * cc, axis=-1, keepdims=True)
        y = cc * lax.rsqrt(var + EPS) * ln_g_ref[...] + ln_b_ref[...]
        u_bf[rows, :] = (_silu(y) * _silu(buf_z[rows, :])).astype(BF16)
    _tile_loop(n_tiles, ln_body)

    buf2[...] = jnp.dot(u_bf[...], wco_ref[...], preferred_element_type=F32)

    def gate_conv_body(i):
        rows = _rows(i, ROW_TILE)
        buf2[rows, :] = _sigmoid(buf_mc[rows, :]) * buf2[rows, :]
    _tile_loop(n_tiles, gate_conv_body)

    def gn_body(i):
        rows = _rows(i, ROW_TILE)
        for h in range(HEADS):
            lanes = slice(h * HEAD_DIM, (h + 1) * HEAD_DIM)
            o = oh[h, rows, :]
            ms = jnp.mean(o * o, axis=-1, keepdims=True)
            on = o * lax.rsqrt(ms + EPS) * gn_ref[:, lanes]
            u_bf[rows, lanes] = (on * _silu(buf_g[rows, lanes])).astype(BF16)
    _tile_loop(n_tiles, gn_body)

    buf3[...] = jnp.dot(u_bf[...], wro_ref[...], preferred_element_type=F32)

    def merge_body(i):
        rows = _rows(i, ROW_TILE)
        merged = buf2[rows, :] + _sigmoid(buf_mr[rows, :]) * buf3[rows, :]
        u_bf[rows, :] = merged.astype(BF16)
    _tile_loop(n_tiles, merge_body)

    h_nat[...] = jnp.dot(pmat_t_ref[...], u_bf[...], preferred_element_type=F32).astype(BF16)
    buf3[...] = jnp.dot(h_nat[...], wo_ref[...], preferred_element_type=F32)

    def out_body(i):
        rows = _rows(i, ROW_TILE)
        y = x_ref[0, rows, :] + buf3[rows, :]
        ms = jnp.mean(y * y, axis=-1, keepdims=True)
        out_ref[0, rows, :] = y * lax.rsqrt(ms + EPS) * fg_ref[...]
    _tile_loop(n_tiles, out_body)


def _layout_constants():
    i = np.arange(BLOCK_T)
    token = N_POS * (i % SUBLANES) + i // SUBLANES
    pmat = np.zeros((BLOCK_T, BLOCK_T), np.float32)
    pmat[i, token] = 1.0
    t, s = token[:, None], token[None, :]
    x = np.bitwise_xor(t, s)
    top = np.floor(np.log2(np.maximum(x, 1))).astype(np.int32)
    lvl = np.where(s < t, top, np.where(s == t, DIAG_LEVEL, -1)).astype(np.int32)
    return pmat, lvl


def _const_spec(shape, n_grid=1):
    zeros = (0,) * len(shape)
    index_map = (lambda n: zeros) if n_grid == 1 else (lambda b, j: zeros)
    return pl.BlockSpec(shape, index_map, pipeline_mode=pl.Buffered(1))


def kernel(x, meta_tokens, norm_g, w_in, conv_w, conv_b, ln_g, ln_b, w_conv_out, lb_logits,
           gnorm_g, w_rec_out, w_out, final_g):
    bsz, seq, d = x.shape
    assert d == D_MODEL and seq % BLOCK_T == 0 and N_META <= BLOCK_T
    assert norm_g.shape[0] == 1, "single-layer block"
    blocks_per_seq = seq // BLOCK_T

    pmat_np, lvl_np = _layout_constants()
    pmat = jnp.asarray(pmat_np, BF16)
    pmat_t = jnp.asarray(pmat_np.T, BF16)
    lvl = jnp.asarray(lvl_np, BF16)

    row = lambda v: v.reshape(1, D_MODEL).astype(F32)
    w_in_b = w_in[0].astype(BF16)
    wco_b = w_conv_out[0].astype(BF16)
    wro_b = w_rec_out[0].astype(BF16)
    wo_b = w_out[0].astype(BF16)
    norm_g2, conv_b2, ln_g2, ln_b2 = row(norm_g[0]), row(conv_b[0]), row(ln_g[0]), row(ln_b[0])
    gn2, fg2 = row(gnorm_g[0]), row(final_g)
    def halves(v):
        lo, hi = v[..., None, :D_MODEL // 2], v[..., None, D_MODEL // 2:]
        shape = v.shape[:-1] + (SUBLANES, D_MODEL // 2)
        return jnp.concatenate([jnp.broadcast_to(lo, shape), jnp.broadcast_to(hi, shape)], axis=-2)
    conv_w2 = halves(conv_w[0].astype(F32)).reshape(CONV_WIDTH * PACKED_ROWS, D_MODEL // 2).astype(BF16)
    conv_b2 = halves(conv_b[0].astype(F32))
    lb2 = lb_logits.astype(F32)

    meta_block = jnp.concatenate(
        [jnp.zeros((BLOCK_T - N_META, D_MODEL), x.dtype), meta_tokens.astype(x.dtype)], axis=0)

    big = lambda dt: pltpu.VMEM((BLOCK_T, D_MODEL), dt)
    head_major = lambda dt: pltpu.VMEM((HEADS, BLOCK_T, HEAD_DIM), dt)
    params = lambda n_grid: pltpu.CompilerParams(dimension_semantics=("arbitrary",) * n_grid,
                                                 vmem_limit_bytes=VMEM_LIMIT_BYTES)

    meta_carry, meta_state = pl.pallas_call(
        _meta_kernel,
        grid=(1,),
        in_specs=[_const_spec((BLOCK_T, D_MODEL)), _const_spec(pmat.shape),
                  _const_spec(norm_g2.shape), _const_spec(w_in_b.shape), _const_spec(lb2.shape)],
        out_specs=[pl.BlockSpec((BLOCK_T, D_MODEL), lambda n: (0, 0)),
                   pl.BlockSpec((HEADS, HEAD_DIM, HEAD_DIM), lambda n: (0, 0, 0))],
        out_shape=[jax.ShapeDtypeStruct((BLOCK_T, D_MODEL), F32),
                   jax.ShapeDtypeStruct((HEADS, HEAD_DIM, HEAD_DIM), F32)],
        scratch_shapes=[big(BF16), big(BF16), big(F32), big(F32), head_major(F32), head_major(BF16)],
        compiler_params=params(1),
        name="meta_block",
    )(meta_block, pmat, norm_g2, w_in_b, lb2)

    block_spec = pl.BlockSpec((1, BLOCK_T, D_MODEL), lambda b, j: (b, j, 0))
    consts = [meta_carry, meta_state, pmat, pmat_t, lvl, norm_g2, w_in_b, conv_w2, conv_b2,
              ln_g2, ln_b2, wco_b, lb2, gn2, wro_b, wo_b, fg2]
    out = pl.pallas_call(
        _main_kernel,
        grid=(bsz, blocks_per_seq),
        in_specs=[block_spec] + [_const_spec(c.shape, 2) for c in consts],
        out_specs=block_spec,
        out_shape=jax.ShapeDtypeStruct(x.shape, x.dtype),
        scratch_shapes=[big(BF16), big(BF16), big(F32),
                        head_major(F32), head_major(F32), head_major(BF16)]
                       + [big(F32)] * 8
                       + [pltpu.VMEM((2 * N_POS * PACKED_ROWS, D_MODEL // 2), BF16), big(F32),
                          pltpu.VMEM((HEADS, HEAD_DIM, HEAD_DIM), F32), head_major(F32), big(BF16)],
        compiler_params=params(2),
        name="main_blocks",
    )(x, *consts)
    return out
```

```python
import numpy as np
import jax
import jax.numpy as jnp
from jax import lax
from jax.experimental import pallas as pl
from jax.experimental.pallas import tpu as pltpu

D_MODEL = 1024
N_META = 16
CONV_WIDTH = 31
HEADS = 8
HEAD_DIM = 128
EPS = 1e-6

SUBLANES = 8
BLOCK_T = 256
N_POS = BLOCK_T // SUBLANES
N_LEVELS = 8
DIAG_LEVEL = N_LEVELS
HALO_ROWS = N_POS * SUBLANES
ROW_TILE = 16
CONV_ROWS = 32
VMEM_LIMIT_BYTES = 56 * 1024 * 1024
assert N_POS >= CONV_WIDTH - 1

F32 = jnp.float32
BF16 = jnp.bfloat16
NT_DIMS = (((1,), (1,)), ((), ()))
TN_DIMS = (((0,), (0,)), ((), ()))

_C_GLU_A, _C_GLU_B, _C_Z, _C_Q, _C_F, _C_I, _C_G, _C_MC, _C_MR = [i * D_MODEL for i in range(9)]


def _sigmoid(x):
    return jax.nn.sigmoid(x)


def _silu(x):
    return x * jax.nn.sigmoid(x)


def _rows(i, n):
    return slice(i * n, (i + 1) * n)


def _tile_loop(n, body):
    for i in range(n):
        body(i)


def _proj(h_ref, w_ref, col):
    return jnp.dot(h_ref[...], w_ref[:, col:col + D_MODEL], preferred_element_type=F32)


def _sub_roll(x, shift):
    return pltpu.roll(x, shift % SUBLANES, 0)


def _cat_bf16(tiles):
    return jnp.concatenate(tiles, axis=0).astype(BF16)


def _head_levels(q_t, k_t, f_t, want_q):
    fwd = list(f_t)
    rev = [None] * N_POS
    levels = []

    def times(x_t, s_t):
        return [x if s is None else x * s for x, s in zip(x_t, s_t)]

    n = 1
    while n < N_POS:
        levels.append((_cat_bf16(times(q_t, fwd)) if want_q else None, _cat_bf16(times(k_t, rev))))
        new_fwd, new_rev = list(fwd), list(rev)
        for b0 in range(0, N_POS, 2 * n):
            tot_lo, tot_up = fwd[b0 + n - 1], fwd[b0 + 2 * n - 1]
            for i in range(n):
                new_fwd[b0 + n + i] = fwd[b0 + n + i] * tot_lo
                new_rev[b0 + i] = tot_up if rev[b0 + i] is None else rev[b0 + i] * tot_up
        fwd, rev = new_fwd, new_rev
        n *= 2

    g = fwd[N_POS - 1]
    sub = lax.broadcasted_iota(jnp.int32, g.shape, 0)
    sub4 = sub % 4
    one = jnp.ones_like(g)
    gm = [None] + [_sub_roll(g, s) for s in range(1, SUBLANES)]
    gp = [None] + [_sub_roll(g, -s) for s in range(1, SUBLANES)]
    u64 = jnp.where(sub4 == 3, gm[1], one)
    w64 = jnp.where(sub4 == 0, gp[1], one)
    u128 = (jnp.where(sub4 >= 1, gm[1], one) * jnp.where(sub4 >= 2, gm[2], one)
            * jnp.where(sub4 >= 3, gm[3], one))
    w128 = (jnp.where(sub4 <= 2, gp[1], one) * jnp.where(sub4 <= 1, gp[2], one)
            * jnp.where(sub4 == 0, gp[3], one))
    px, sx = one, one
    for s in range(1, SUBLANES):
        px = px * jnp.where(sub >= s, gm[s], one)
        sx = sx * jnp.where(sub <= SUBLANES - 1 - s, gp[s], one)
    g_total = (px * g * sx)[0:1, :]

    k_rev = times(k_t, rev)
    q_fwd = times(q_t, fwd) if want_q else None

    def scaled(x_t, s):
        return _cat_bf16([x * s for x in x_t])

    levels.append((_cat_bf16(q_fwd) if want_q else None, _cat_bf16(k_rev)))
    levels.append((scaled(q_fwd, u64) if want_q else None, scaled(k_rev, w64)))
    levels.append((scaled(q_fwd, u128) if want_q else None, scaled(k_rev, w128)))
    levels.append((_cat_bf16(q_t) if want_q else None, _cat_bf16(k_t)))
    q_inter = scaled(q_fwd, px) if want_q else None
    k_state = scaled(k_rev, sx)
    return levels, q_inter, k_state, g_total


def _norm_and_permute(x_rows, norm_g_ref, pmat_ref, h_nat, h_p):
    def body(i):
        rows = _rows(i, ROW_TILE)
        xv = x_rows(rows)
        ms = jnp.mean(xv * xv, axis=-1, keepdims=True)
        h_nat[rows, :] = (xv * lax.rsqrt(ms + EPS) * norm_g_ref[...]).astype(BF16)
    _tile_loop(BLOCK_T // ROW_TILE, body)
    h_p[...] = jnp.dot(pmat_ref[...], h_nat[...], preferred_element_type=F32).astype(BF16)


def _glu_and_history(glu_a, glu_b, carry, a_ext):
    def body(i):
        rows = _rows(i, SUBLANES)
        a = glu_a[rows, :] * _sigmoid(glu_b[rows, :])
        rolled = pltpu.roll(a, 1, 0)
        if a_ext is not None:
            sub = lax.broadcasted_iota(jnp.int32, a.shape, 0)
            a_ext[_rows(i + N_POS, SUBLANES), :] = a
            a_ext[rows, :] = jnp.where(sub == 0, carry[rows, :], rolled)
        carry[rows, :] = rolled
    _tile_loop(N_POS, body)


def _causal_conv(a_ext, conv_w_ref, conv_b_ref, out):
    span = CONV_ROWS + SUBLANES * (CONV_WIDTH - 1)
    for i in range(BLOCK_T // CONV_ROWS):
        first = i * CONV_ROWS + SUBLANES * (N_POS - CONV_WIDTH + 1)
        for l in range(D_MODEL // HEAD_DIM):
            lanes = slice(l * HEAD_DIM, (l + 1) * HEAD_DIM)
            win = a_ext[first:first + span, lanes]
            acc = jnp.broadcast_to(conv_b_ref[:, lanes], (CONV_ROWS, HEAD_DIM))
            for j in range(CONV_WIDTH):
                acc = acc + conv_w_ref[j:j + 1, lanes] * win[SUBLANES * j:SUBLANES * j + CONV_ROWS]
            out[_rows(i, CONV_ROWS), lanes] = acc


def _recurrence_inputs(h_p, w_in_ref, lb_ref, buf, qh, fh, vh, pad_front):
    l0, l1 = lb_ref[0:1, :], lb_ref[1:2, :]
    lmax = jnp.maximum(l0, l1)
    e0, e1 = jnp.exp(l0 - lmax), jnp.exp(l1 - lmax)
    lb = e0 / (e0 + e1)
    n_tiles = BLOCK_T // ROW_TILE

    def to_heads(dst, val, rows):
        for h in range(HEADS):
            dst[h, rows, :] = val[:, h * HEAD_DIM:(h + 1) * HEAD_DIM]

    if qh is not None:
        buf[...] = _proj(h_p, w_in_ref, _C_Q)
        _tile_loop(n_tiles, lambda i: to_heads(qh, _silu(buf[_rows(i, ROW_TILE), :]),
                                               _rows(i, ROW_TILE)))

    buf[...] = _proj(h_p, w_in_ref, _C_F)

    def f_body(i):
        rows = _rows(i, ROW_TILE)
        fv = lb + (1.0 - lb) * _sigmoid(buf[rows, :])
        if pad_front:
            row = lax.broadcasted_iota(jnp.int32, fv.shape, 0) + i * ROW_TILE
            token = N_POS * (row % SUBLANES) + row // SUBLANES
            fv = jnp.where(token < BLOCK_T - N_META, 1.0, fv)
        to_heads(fh, fv, rows)
    _tile_loop(n_tiles, f_body)

    buf[...] = _proj(h_p, w_in_ref, _C_I)
    _tile_loop(n_tiles, lambda i: to_heads(vh, buf[_rows(i, ROW_TILE), :].astype(BF16),
                                           _rows(i, ROW_TILE)))


def _recurrence_head(h, qh, fh, vh, st, lvl_ref, oh):
    want_q = qh is not None
    f_t = [fh[h, _rows(a, SUBLANES), :] for a in range(N_POS)]
    k_t = [1.0 - f for f in f_t]
    q_t = [qh[h, _rows(a, SUBLANES), :] for a in range(N_POS)] if want_q else None
    levels, q_inter, k_state, g_total = _head_levels(q_t, k_t, f_t, want_q)
    v = vh[h]
    upd = lax.dot_general(v, k_state, TN_DIMS, preferred_element_type=F32)
    if not want_q:
        st[h] = upd
        return
    s_t = st[h]
    lvl = lvl_ref[...]
    scores = jnp.zeros((BLOCK_T, BLOCK_T), F32)
    for l, (ql, kl) in enumerate(levels):
        part = lax.dot_general(ql, kl, NT_DIMS, preferred_element_type=F32)
        scores = jnp.where(lvl == l, part, scores)
    o = jnp.dot(scores.astype(BF16), v, preferred_element_type=F32)
    o = o + lax.dot_general(q_inter, s_t.astype(BF16), NT_DIMS, preferred_element_type=F32)
    oh[h] = o
    st[h] = s_t * g_total + upd


def _meta_kernel(x_ref, pmat_ref, norm_g_ref, w_in_ref, lb_ref, carry_out, st_out,
                 h_nat, h_p, buf0, buf1, fh, vh):
    _norm_and_permute(lambda rows: x_ref[rows, :], norm_g_ref, pmat_ref, h_nat, h_p)
    buf0[...] = _proj(h_p, w_in_ref, _C_GLU_A)
    buf1[...] = _proj(h_p, w_in_ref, _C_GLU_B)
    _glu_and_history(buf0, buf1, carry_out, None)
    _recurrence_inputs(h_p, w_in_ref, lb_ref, buf1, None, fh, vh, pad_front=True)
    for h in range(HEADS):
        _recurrence_head(h, None, fh, vh, st_out, None, None)


def _main_kernel(x_ref, mcarry_ref, mst_ref, pmat_ref, pmat_t_ref, lvl_ref, norm_g_ref,
                 w_in_ref, conv_w_ref, conv_b_ref, ln_g_ref, ln_b_ref, wco_ref, lb_ref, gn_ref,
                 wro_ref, wo_ref, fg_ref, out_ref,
                 h_nat, h_p, conv_out, qh, fh, vh,
                 buf0, buf1, buf2, buf3, buf_z, buf_mc, buf_g, buf_mr,
                 a_ext, carry, st, oh, u_bf):
    @pl.when(pl.program_id(1) == 0)
    def _():
        carry[...] = mcarry_ref[...]
        st[...] = mst_ref[...]

    n_tiles = BLOCK_T // ROW_TILE

    _norm_and_permute(lambda rows: x_ref[0, rows, :], norm_g_ref, pmat_ref, h_nat, h_p)
    buf0[...] = _proj(h_p, w_in_ref, _C_GLU_A)
    buf1[...] = _proj(h_p, w_in_ref, _C_GLU_B)
    _glu_and_history(buf0, buf1, carry, a_ext)
    _causal_conv(a_ext, conv_w_ref, conv_b_ref, conv_out)
    _recurrence_inputs(h_p, w_in_ref, lb_ref, buf1, qh, fh, vh, pad_front=False)

    buf_z[...] = _proj(h_p, w_in_ref, _C_Z)
    buf_mc[...] = _proj(h_p, w_in_ref, _C_MC)
    buf_g[...] = _proj(h_p, w_in_ref, _C_G)
    buf_mr[...] = _proj(h_p, w_in_ref, _C_MR)

    for h in range(HEADS):
        _recurrence_head(h, qh, fh, vh, st, lvl_ref, oh)

    def ln_body(i):
        rows = _rows(i, ROW_TILE)
        cv = conv_out[rows, :]
        mu = jnp.mean(cv, axis=-1, keepdims=True)
        cc = cv - mu
        var = jnp.mean(cc * cc, axis=-1, keepdims=True)
        y = cc * lax.rsqrt(var + EPS) * ln_g_ref[...] + ln_b_ref[...]
        u_bf[rows, :] = (_silu(y) * _silu(buf_z[rows, :])).astype(BF16)
    _tile_loop(n_tiles, ln_body)

    buf2[...] = jnp.dot(u_bf[...], wco_ref[...], preferred_element_type=F32)

    def gate_conv_body(i):
        rows = _rows(i, ROW_TILE)
        buf2[rows, :] = _sigmoid(buf_mc[rows, :]) * buf2[rows, :]
    _tile_loop(n_tiles, gate_conv_body)

    def gn_body(i):
        rows = _rows(i, ROW_TILE)
        for h in range(HEADS):
            lanes = slice(h * HEAD_DIM, (h + 1) * HEAD_DIM)
            o = oh[h, rows, :]
            ms = jnp.mean(o * o, axis=-1, keepdims=True)
            on = o * lax.rsqrt(ms + EPS) * gn_ref[:, lanes]
            u_bf[rows, lanes] = (on * _silu(buf_g[rows, lanes])).astype(BF16)
    _tile_loop(n_tiles, gn_body)

    buf3[...] = jnp.dot(u_bf[...], wro_ref[...], preferred_element_type=F32)

    def merge_body(i):
        rows = _rows(i, ROW_TILE)
        merged = buf2[rows, :] + _sigmoid(buf_mr[rows, :]) * buf3[rows, :]
        u_bf[rows, :] = merged.astype(BF16)
    _tile_loop(n_tiles, merge_body)

    h_nat[...] = jnp.dot(pmat_t_ref[...], u_bf[...], preferred_element_type=F32).astype(BF16)
    buf3[...] = jnp.dot(h_nat[...], wo_ref[...], preferred_element_type=F32)

    def out_body(i):
        rows = _rows(i, ROW_TILE)
        y = x_ref[0, rows, :] + buf3[rows, :]
        ms = jnp.mean(y * y, axis=-1, keepdims=True)
        out_ref[0, rows, :] = y * lax.rsqrt(ms + EPS) * fg_ref[...]
    _tile_loop(n_tiles, out_body)


def _layout_constants():
    i = np.arange(BLOCK_T)
    token = N_POS * (i % SUBLANES) + i // SUBLANES
    pmat = np.zeros((BLOCK_T, BLOCK_T), np.float32)
    pmat[i, token] = 1.0
    t, s = token[:, None], token[None, :]
    x = np.bitwise_xor(t, s)
    top = np.floor(np.log2(np.maximum(x, 1))).astype(np.int32)
    lvl = np.where(s < t, top, np.where(s == t, DIAG_LEVEL, -1)).astype(np.int32)
    return pmat, lvl


def _const_spec(shape, n_grid=1):
    zeros = (0,) * len(shape)
    index_map = (lambda n: zeros) if n_grid == 1 else (lambda b, j: zeros)
    return pl.BlockSpec(shape, index_map, pipeline_mode=pl.Buffered(1))


def kernel(x, meta_tokens, norm_g, w_in, conv_w, conv_b, ln_g, ln_b, w_conv_out, lb_logits,
           gnorm_g, w_rec_out, w_out, final_g):
    bsz, seq, d = x.shape
    assert d == D_MODEL and seq % BLOCK_T == 0 and N_META <= BLOCK_T
    assert norm_g.shape[0] == 1, "single-layer block"
    blocks_per_seq = seq // BLOCK_T

    pmat_np, lvl_np = _layout_constants()
    pmat = jnp.asarray(pmat_np, BF16)
    pmat_t = jnp.asarray(pmat_np.T, BF16)
    lvl = jnp.asarray(lvl_np)

    row = lambda v: v.reshape(1, D_MODEL).astype(F32)
    w_in_b = w_in[0].astype(BF16)
    wco_b = w_conv_out[0].astype(BF16)
    wro_b = w_rec_out[0].astype(BF16)
    wo_b = w_out[0].astype(BF16)
    norm_g2, conv_b2, ln_g2, ln_b2 = row(norm_g[0]), row(conv_b[0]), row(ln_g[0]), row(ln_b[0])
    gn2, fg2 = row(gnorm_g[0]), row(final_g)
    conv_w2 = conv_w[0].astype(F32)
    lb2 = lb_logits.astype(F32)

    meta_block = jnp.concatenate(
        [jnp.zeros((BLOCK_T - N_META, D_MODEL), x.dtype), meta_tokens.astype(x.dtype)], axis=0)

    big = lambda dt: pltpu.VMEM((BLOCK_T, D_MODEL), dt)
    head_major = lambda dt: pltpu.VMEM((HEADS, BLOCK_T, HEAD_DIM), dt)
    params = lambda n_grid: pltpu.CompilerParams(dimension_semantics=("arbitrary",) * n_grid,
                                                 vmem_limit_bytes=VMEM_LIMIT_BYTES)

    meta_carry, meta_state = pl.pallas_call(
        _meta_kernel,
        grid=(1,),
        in_specs=[_const_spec((BLOCK_T, D_MODEL)), _const_spec(pmat.shape),
                  _const_spec(norm_g2.shape), _const_spec(w_in_b.shape), _const_spec(lb2.shape)],
        out_specs=[pl.BlockSpec((BLOCK_T, D_MODEL), lambda n: (0, 0)),
                   pl.BlockSpec((HEADS, HEAD_DIM, HEAD_DIM), lambda n: (0, 0, 0))],
        out_shape=[jax.ShapeDtypeStruct((BLOCK_T, D_MODEL), F32),
                   jax.ShapeDtypeStruct((HEADS, HEAD_DIM, HEAD_DIM), F32)],
        scratch_shapes=[big(BF16), big(BF16), big(F32), big(F32), head_major(F32), head_major(BF16)],
        compiler_params=params(1),
        name="meta_block",
    )(meta_block, pmat, norm_g2, w_in_b, lb2)

    block_spec = pl.BlockSpec((1, BLOCK_T, D_MODEL), lambda b, j: (b, j, 0))
    consts = [meta_carry, meta_state, pmat, pmat_t, lvl, norm_g2, w_in_b, conv_w2, conv_b2,
              ln_g2, ln_b2, wco_b, lb2, gn2, wro_b, wo_b, fg2]
    out = pl.pallas_call(
        _main_kernel,
        grid=(bsz, blocks_per_seq),
        in_specs=[block_spec] + [_const_spec(c.shape, 2) for c in consts],
        out_specs=block_spec,
        out_shape=jax.ShapeDtypeStruct(x.shape, x.dtype),
        scratch_shapes=[big(BF16), big(BF16), big(F32),
                        head_major(F32), head_major(F32), head_major(BF16)]
                       + [big(F32)] * 8
                       + [pltpu.VMEM((HALO_ROWS + BLOCK_T, D_MODEL), F32), big(F32),
                          pltpu.VMEM((HEADS, HEAD_DIM, HEAD_DIM), F32), head_major(F32), big(BF16)],
        compiler_params=params(2),
        name="main_blocks",
    )(x, *consts)
    return out
```

```python
import numpy as np
import jax
import jax.numpy as jnp
from jax import lax
from jax.experimental import pallas as pl
from jax.experimental.pallas import tpu as pltpu

D_MODEL = 1024
N_META = 16
CONV_WIDTH = 31
HEADS = 8
HEAD_DIM = 128
EPS = 1e-6

SUBLANES = 8
BLOCK_T = 256
N_POS = BLOCK_T // SUBLANES
N_LEVELS = 8
DIAG_LEVEL = N_LEVELS
HALO_ROWS = N_POS * SUBLANES
ROW_TILE = 16
CONV_ROWS = 32
VMEM_LIMIT_BYTES = 56 * 1024 * 1024
assert N_POS >= CONV_WIDTH - 1

F32 = jnp.float32
BF16 = jnp.bfloat16
NT_DIMS = (((1,), (1,)), ((), ()))
TN_DIMS = (((0,), (0,)), ((), ()))

_C_GLU_A, _C_GLU_B, _C_Z, _C_Q, _C_F, _C_I, _C_G, _C_MC, _C_MR = [i * D_MODEL for i in range(9)]


def _sigmoid(x):
    return jax.nn.sigmoid(x)


def _silu(x):
    return x * jax.nn.sigmoid(x)


def _rows(i, n):
    return slice(i * n, (i + 1) * n)


def _tile_loop(n, body):
    for i in range(n):
        body(i)


def _proj(h_ref, w_ref, col):
    return jnp.dot(h_ref[...], w_ref[:, col:col + D_MODEL], preferred_element_type=F32)


def _sub_roll(x, shift):
    return pltpu.roll(x, shift % SUBLANES, 0)


def _cat_bf16(tiles):
    return jnp.concatenate(tiles, axis=0).astype(BF16)


def _head_levels(q_t, k_t, f_t, want_q):
    fwd = list(f_t)
    rev = [None] * N_POS
    levels = []

    def times(x_t, s_t):
        return [x if s is None else x * s for x, s in zip(x_t, s_t)]

    n = 1
    while n < N_POS:
        levels.append((_cat_bf16(times(q_t, fwd)) if want_q else None, _cat_bf16(times(k_t, rev))))
        new_fwd, new_rev = list(fwd), list(rev)
        for b0 in range(0, N_POS, 2 * n):
            tot_lo, tot_up = fwd[b0 + n - 1], fwd[b0 + 2 * n - 1]
            for i in range(n):
                new_fwd[b0 + n + i] = fwd[b0 + n + i] * tot_lo
                new_rev[b0 + i] = tot_up if rev[b0 + i] is None else rev[b0 + i] * tot_up
        fwd, rev = new_fwd, new_rev
        n *= 2

    g = fwd[N_POS - 1]
    sub = lax.broadcasted_iota(jnp.int32, g.shape, 0)
    sub4 = sub % 4
    one = jnp.ones_like(g)
    gm = [None] + [_sub_roll(g, s) for s in range(1, SUBLANES)]
    gp = [None] + [_sub_roll(g, -s) for s in range(1, SUBLANES)]
    u64 = jnp.where(sub4 == 3, gm[1], one)
    w64 = jnp.where(sub4 == 0, gp[1], one)
    u128 = (jnp.where(sub4 >= 1, gm[1], one) * jnp.where(sub4 >= 2, gm[2], one)
            * jnp.where(sub4 >= 3, gm[3], one))
    w128 = (jnp.where(sub4 <= 2, gp[1], one) * jnp.where(sub4 <= 1, gp[2], one)
            * jnp.where(sub4 == 0, gp[3], one))
    px, sx = one, one
    for s in range(1, SUBLANES):
        px = px * jnp.where(sub >= s, gm[s], one)
        sx = sx * jnp.where(sub <= SUBLANES - 1 - s, gp[s], one)
    g_total = (px * g * sx)[0:1, :]

    k_rev = times(k_t, rev)
    q_fwd = times(q_t, fwd) if want_q else None

    def scaled(x_t, s):
        return _cat_bf16([x * s for x in x_t])

    levels.append((_cat_bf16(q_fwd) if want_q else None, _cat_bf16(k_rev)))
    levels.append((scaled(q_fwd, u64) if want_q else None, scaled(k_rev, w64)))
    levels.append((scaled(q_fwd, u128) if want_q else None, scaled(k_rev, w128)))
    levels.append((_cat_bf16(q_t) if want_q else None, _cat_bf16(k_t)))
    q_inter = scaled(q_fwd, px) if want_q else None
    k_state = scaled(k_rev, sx)
    return levels, q_inter, k_state, g_total


def _norm_and_permute(x_rows, norm_g_ref, pmat_ref, h_nat, h_p):
    def body(i):
        rows = _rows(i, ROW_TILE)
        xv = x_rows(rows)
        ms = jnp.mean(xv * xv, axis=-1, keepdims=True)
        h_nat[rows, :] = (xv * lax.rsqrt(ms + EPS) * norm_g_ref[...]).astype(BF16)
    _tile_loop(BLOCK_T // ROW_TILE, body)
    h_p[...] = jnp.dot(pmat_ref[...], h_nat[...], preferred_element_type=F32).astype(BF16)


def _glu_and_history(glu_a, glu_b, carry, a_ext):
    def body(i):
        rows = _rows(i, SUBLANES)
        a = glu_a[rows, :] * _sigmoid(glu_b[rows, :])
        rolled = pltpu.roll(a, 1, 0)
        if a_ext is not None:
            sub = lax.broadcasted_iota(jnp.int32, a.shape, 0)
            a_ext[_rows(i + N_POS, SUBLANES), :] = a
            a_ext[rows, :] = jnp.where(sub == 0, carry[rows, :], rolled)
        carry[rows, :] = rolled
    _tile_loop(N_POS, body)


def _causal_conv(a_ext, conv_w_ref, conv_b_ref, out):
    span = CONV_ROWS + SUBLANES * (CONV_WIDTH - 1)
    for i in range(BLOCK_T // CONV_ROWS):
        first = i * CONV_ROWS + SUBLANES * (N_POS - CONV_WIDTH + 1)
        for l in range(D_MODEL // HEAD_DIM):
            lanes = slice(l * HEAD_DIM, (l + 1) * HEAD_DIM)
            win = a_ext[first:first + span, lanes]
            acc = jnp.broadcast_to(conv_b_ref[:, lanes], (CONV_ROWS, HEAD_DIM))
            for j in range(CONV_WIDTH):
                acc = acc + conv_w_ref[j:j + 1, lanes] * win[SUBLANES * j:SUBLANES * j + CONV_ROWS]
            out[_rows(i, CONV_ROWS), lanes] = acc


def _recurrence_inputs(h_p, w_in_ref, lb_ref, buf, qh, fh, vh, pad_front):
    l0, l1 = lb_ref[0:1, :], lb_ref[1:2, :]
    lmax = jnp.maximum(l0, l1)
    e0, e1 = jnp.exp(l0 - lmax), jnp.exp(l1 - lmax)
    lb = e0 / (e0 + e1)
    n_tiles = BLOCK_T // ROW_TILE

    def to_heads(dst, val, rows):
        for h in range(HEADS):
            dst[h, rows, :] = val[:, h * HEAD_DIM:(h + 1) * HEAD_DIM]

    if qh is not None:
        buf[...] = _proj(h_p, w_in_ref, _C_Q)
        _tile_loop(n_tiles, lambda i: to_heads(qh, _silu(buf[_rows(i, ROW_TILE), :]),
                                               _rows(i, ROW_TILE)))

    buf[...] = _proj(h_p, w_in_ref, _C_F)

    def f_body(i):
        rows = _rows(i, ROW_TILE)
        fv = lb + (1.0 - lb) * _sigmoid(buf[rows, :])
        if pad_front:
            row = lax.broadcasted_iota(jnp.int32, fv.shape, 0) + i * ROW_TILE
            token = N_POS * (row % SUBLANES) + row // SUBLANES
            fv = jnp.where(token < BLOCK_T - N_META, 1.0, fv)
        to_heads(fh, fv, rows)
    _tile_loop(n_tiles, f_body)

    buf[...] = _proj(h_p, w_in_ref, _C_I)
    _tile_loop(n_tiles, lambda i: to_heads(vh, buf[_rows(i, ROW_TILE), :].astype(BF16),
                                           _rows(i, ROW_TILE)))


def _recurrence_head(h, qh, fh, vh, st, lvl_ref, oh):
    want_q = qh is not None
    f_t = [fh[h, _rows(a, SUBLANES), :] for a in range(N_POS)]
    k_t = [1.0 - f for f in f_t]
    q_t = [qh[h, _rows(a, SUBLANES), :] for a in range(N_POS)] if want_q else None
    levels, q_inter, k_state, g_total = _head_levels(q_t, k_t, f_t, want_q)
    v = vh[h]
    upd = lax.dot_general(v, k_state, TN_DIMS, preferred_element_type=F32)
    if not want_q:
        st[h] = upd
        return
    s_t = st[h]
    lvl = lvl_ref[...]
    scores = jnp.zeros((BLOCK_T, BLOCK_T), BF16)
    for l, (ql, kl) in enumerate(levels):
        part = lax.dot_general(ql, kl, NT_DIMS, preferred_element_type=F32)
        scores = jnp.where(lvl == l, part.astype(BF16), scores)
    o = jnp.dot(scores, v, preferred_element_type=F32)
    o = o + lax.dot_general(q_inter, s_t.astype(BF16), NT_DIMS, preferred_element_type=F32)
    oh[h] = o
    st[h] = s_t * g_total + upd


def _meta_kernel(x_ref, pmat_ref, norm_g_ref, w_in_ref, lb_ref, carry_out, st_out,
                 h_nat, h_p, buf0, buf1, fh, vh):
    _norm_and_permute(lambda rows: x_ref[rows, :], norm_g_ref, pmat_ref, h_nat, h_p)
    buf0[...] = _proj(h_p, w_in_ref, _C_GLU_A)
    buf1[...] = _proj(h_p, w_in_ref, _C_GLU_B)
    _glu_and_history(buf0, buf1, carry_out, None)
    _recurrence_inputs(h_p, w_in_ref, lb_ref, buf1, None, fh, vh, pad_front=True)
    for h in range(HEADS):
        _recurrence_head(h, None, fh, vh, st_out, None, None)


def _main_kernel(x_ref, mcarry_ref, mst_ref, pmat_ref, pmat_t_ref, lvl_ref, norm_g_ref,
                 w_in_ref, conv_w_ref, conv_b_ref, ln_g_ref, ln_b_ref, wco_ref, lb_ref, gn_ref,
                 wro_ref, wo_ref, fg_ref, out_ref,
                 h_nat, h_p, conv_out, qh, fh, vh,
                 buf0, buf1, buf2, buf3, buf_z, buf_mc, buf_g, buf_mr,
                 a_ext, carry, st, oh, u_bf):
    @pl.when(pl.program_id(1) == 0)
    def _():
        carry[...] = mcarry_ref[...]
        st[...] = mst_ref[...]

    n_tiles = BLOCK_T // ROW_TILE

    _norm_and_permute(lambda rows: x_ref[0, rows, :], norm_g_ref, pmat_ref, h_nat, h_p)
    buf0[...] = _proj(h_p, w_in_ref, _C_GLU_A)
    buf1[...] = _proj(h_p, w_in_ref, _C_GLU_B)
    _glu_and_history(buf0, buf1, carry, a_ext)
    _causal_conv(a_ext, conv_w_ref, conv_b_ref, conv_out)
    _recurrence_inputs(h_p, w_in_ref, lb_ref, buf1, qh, fh, vh, pad_front=False)

    buf_z[...] = _proj(h_p, w_in_ref, _C_Z)
    buf_mc[...] = _proj(h_p, w_in_ref, _C_MC)
    buf_g[...] = _proj(h_p, w_in_ref, _C_G)
    buf_mr[...] = _proj(h_p, w_in_ref, _C_MR)

    for h in range(HEADS):
        _recurrence_head(h, qh, fh, vh, st, lvl_ref, oh)

    def ln_body(i):
        rows = _rows(i, ROW_TILE)
        cv = conv_out[rows, :]
        mu = jnp.mean(cv, axis=-1, keepdims=True)
        cc = cv - mu
        var = jnp.mean(cc * cc, axis=-1, keepdims=True)
        y = cc * lax.rsqrt(var + EPS) * ln_g_ref[...] + ln_b_ref[...]
        u_bf[rows, :] = (_silu(y) * _silu(buf_z[rows, :])).astype(BF16)
    _tile_loop(n_tiles, ln_body)

    buf2[...] = jnp.dot(u_bf[...], wco_ref[...], preferred_element_type=F32)

    def gate_conv_body(i):
        rows = _rows(i, ROW_TILE)
        buf2[rows, :] = _sigmoid(buf_mc[rows, :]) * buf2[rows, :]
    _tile_loop(n_tiles, gate_conv_body)

    def gn_body(i):
        rows = _rows(i, ROW_TILE)
        for h in range(HEADS):
            lanes = slice(h * HEAD_DIM, (h + 1) * HEAD_DIM)
            o = oh[h, rows, :]
            ms = jnp.mean(o * o, axis=-1, keepdims=True)
            on = o * lax.rsqrt(ms + EPS) * gn_ref[:, lanes]
            u_bf[rows, lanes] = (on * _silu(buf_g[rows, lanes])).astype(BF16)
    _tile_loop(n_tiles, gn_body)

    buf3[...] = jnp.dot(u_bf[...], wro_ref[...], preferred_element_type=F32)

    def merge_body(i):
        rows = _rows(i, ROW_TILE)
        merged = buf2[rows, :] + _sigmoid(buf_mr[rows, :]) * buf3[rows, :]
        u_bf[rows, :] = merged.astype(BF16)
    _tile_loop(n_tiles, merge_body)

    h_nat[...] = jnp.dot(pmat_t_ref[...], u_bf[...], preferred_element_type=F32).astype(BF16)
    buf3[...] = jnp.dot(h_nat[...], wo_ref[...], preferred_element_type=F32)

    def out_body(i):
        rows = _rows(i, ROW_TILE)
        y = x_ref[0, rows, :] + buf3[rows, :]
        ms = jnp.mean(y * y, axis=-1, keepdims=True)
        out_ref[0, rows, :] = y * lax.rsqrt(ms + EPS) * fg_ref[...]
    _tile_loop(n_tiles, out_body)


def _layout_constants():
    i = np.arange(BLOCK_T)
    token = N_POS * (i % SUBLANES) + i // SUBLANES
    pmat = np.zeros((BLOCK_T, BLOCK_T), np.float32)
    pmat[i, token] = 1.0
    t, s = token[:, None], token[None, :]
    x = np.bitwise_xor(t, s)
    top = np.floor(np.log2(np.maximum(x, 1))).astype(np.int32)
    lvl = np.where(s < t, top, np.where(s == t, DIAG_LEVEL, -1)).astype(np.int32)
    return pmat, lvl


def _const_spec(shape, n_grid=1):
    zeros = (0,) * len(shape)
    index_map = (lambda n: zeros) if n_grid == 1 else (lambda b, j: zeros)
    return pl.BlockSpec(shape, index_map, pipeline_mode=pl.Buffered(1))


def kernel(x, meta_tokens, norm_g, w_in, conv_w, conv_b, ln_g, ln_b, w_conv_out, lb_logits,
           gnorm_g, w_rec_out, w_out, final_g):
    bsz, seq, d = x.shape
    assert d == D_MODEL and seq % BLOCK_T == 0 and N_META <= BLOCK_T
    assert norm_g.shape[0] == 1, "single-layer block"
    blocks_per_seq = seq // BLOCK_T

    pmat_np, lvl_np = _layout_constants()
    pmat = jnp.asarray(pmat_np, BF16)
    pmat_t = jnp.asarray(pmat_np.T, BF16)
    lvl = jnp.asarray(lvl_np, BF16)

    row = lambda v: v.reshape(1, D_MODEL).astype(F32)
    w_in_b = w_in[0].astype(BF16)
    wco_b = w_conv_out[0].astype(BF16)
    wro_b = w_rec_out[0].astype(BF16)
    wo_b = w_out[0].astype(BF16)
    norm_g2, conv_b2, ln_g2, ln_b2 = row(norm_g[0]), row(conv_b[0]), row(ln_g[0]), row(ln_b[0])
    gn2, fg2 = row(gnorm_g[0]), row(final_g)
    conv_w2 = conv_w[0].astype(F32)
    lb2 = lb_logits.astype(F32)

    meta_block = jnp.concatenate(
        [jnp.zeros((BLOCK_T - N_META, D_MODEL), x.dtype), meta_tokens.astype(x.dtype)], axis=0)

    big = lambda dt: pltpu.VMEM((BLOCK_T, D_MODEL), dt)
    head_major = lambda dt: pltpu.VMEM((HEADS, BLOCK_T, HEAD_DIM), dt)
    params = lambda n_grid: pltpu.CompilerParams(dimension_semantics=("arbitrary",) * n_grid,
                                                 vmem_limit_bytes=VMEM_LIMIT_BYTES)

    meta_carry, meta_state = pl.pallas_call(
        _meta_kernel,
        grid=(1,),
        in_specs=[_const_spec((BLOCK_T, D_MODEL)), _const_spec(pmat.shape),
                  _const_spec(norm_g2.shape), _const_spec(w_in_b.shape), _const_spec(lb2.shape)],
        out_specs=[pl.BlockSpec((BLOCK_T, D_MODEL), lambda n: (0, 0)),
                   pl.BlockSpec((HEADS, HEAD_DIM, HEAD_DIM), lambda n: (0, 0, 0))],
        out_shape=[jax.ShapeDtypeStruct((BLOCK_T, D_MODEL), F32),
                   jax.ShapeDtypeStruct((HEADS, HEAD_DIM, HEAD_DIM), F32)],
        scratch_shapes=[big(BF16), big(BF16), big(F32), big(F32), head_major(F32), head_major(BF16)],
        compiler_params=params(1),
        name="meta_block",
    )(meta_block, pmat, norm_g2, w_in_b, lb2)

    block_spec = pl.BlockSpec((1, BLOCK_T, D_MODEL), lambda b, j: (b, j, 0))
    consts = [meta_carry, meta_state, pmat, pmat_t, lvl, norm_g2, w_in_b, conv_w2, conv_b2,
              ln_g2, ln_b2, wco_b, lb2, gn2, wro_b, wo_b, fg2]
    out = pl.pallas_call(
        _main_kernel,
        grid=(bsz, blocks_per_seq),
        in_specs=[block_spec] + [_const_spec(c.shape, 2) for c in consts],
        out_specs=block_spec,
        out_shape=jax.ShapeDtypeStruct(x.shape, x.dtype),
        scratch_shapes=[big(BF16), big(BF16), big(F32),
                        head_major(F32), head_major(F32), head_major(BF16)]
                       + [big(F32)] * 8
                       + [pltpu.VMEM((HALO_ROWS + BLOCK_T, D_MODEL), F32), big(F32),
                          pltpu.VMEM((HEADS, HEAD_DIM, HEAD_DIM), F32), head_major(F32), big(BF16)],
        compiler_params=params(2),
        name="main_blocks",
    )(x, *consts)
    return out
```

```python
import numpy as np
import jax
import jax.numpy as jnp
from jax import lax
from jax.experimental import pallas as pl
from jax.experimental.pallas import tpu as pltpu

D_MODEL = 1024
N_META = 16
CONV_WIDTH = 31
HEADS = 8
HEAD_DIM = 128
EPS = 1e-6

SUBLANES = 8
BLOCK_T = 256
N_POS = BLOCK_T // SUBLANES
N_LEVELS = 8
DIAG_LEVEL = N_LEVELS
HALO_ROWS = N_POS * SUBLANES
ROW_TILE = 16
CONV_ROWS = 32
VMEM_LIMIT_BYTES = 56 * 1024 * 1024
assert N_POS >= CONV_WIDTH - 1

F32 = jnp.float32
BF16 = jnp.bfloat16
NT_DIMS = (((1,), (1,)), ((), ()))
TN_DIMS = (((0,), (0,)), ((), ()))

_C_GLU_A, _C_GLU_B, _C_Z, _C_Q, _C_F, _C_I, _C_G, _C_MC, _C_MR = [i * D_MODEL for i in range(9)]


def _sigmoid(x):
    return jax.nn.sigmoid(x)


def _silu(x):
    return x * jax.nn.sigmoid(x)


def _rows(i, n):
    return slice(i * n, (i + 1) * n)


def _tile_loop(n, body):
    for i in range(n):
        body(i)


def _proj(h_ref, w_ref, col):
    return jnp.dot(h_ref[...], w_ref[:, col:col + D_MODEL], preferred_element_type=F32)


def _sub_roll(x, shift):
    return pltpu.roll(x, shift % SUBLANES, 0)


def _cat_bf16(tiles):
    return jnp.concatenate(tiles, axis=0).astype(BF16)


def _head_levels(q_t, k_t, f_t, want_q):
    fwd = list(f_t)
    rev = [None] * N_POS
    levels = []

    def times(x_t, s_t):
        return [x if s is None else x * s for x, s in zip(x_t, s_t)]

    n = 1
    while n < N_POS:
        levels.append((_cat_bf16(times(q_t, fwd)) if want_q else None, _cat_bf16(times(k_t, rev))))
        new_fwd, new_rev = list(fwd), list(rev)
        for b0 in range(0, N_POS, 2 * n):
            tot_lo, tot_up = fwd[b0 + n - 1], fwd[b0 + 2 * n - 1]
            for i in range(n):
                new_fwd[b0 + n + i] = fwd[b0 + n + i] * tot_lo
                new_rev[b0 + i] = tot_up if rev[b0 + i] is None else rev[b0 + i] * tot_up
        fwd, rev = new_fwd, new_rev
        n *= 2

    g = fwd[N_POS - 1]
    sub = lax.broadcasted_iota(jnp.int32, g.shape, 0)
    sub4 = sub % 4
    one = jnp.ones_like(g)
    gm = [None] + [_sub_roll(g, s) for s in range(1, SUBLANES)]
    gp = [None] + [_sub_roll(g, -s) for s in range(1, SUBLANES)]
    u64 = jnp.where(sub4 == 3, gm[1], one)
    w64 = jnp.where(sub4 == 0, gp[1], one)
    u128 = (jnp.where(sub4 >= 1, gm[1], one) * jnp.where(sub4 >= 2, gm[2], one)
            * jnp.where(sub4 >= 3, gm[3], one))
    w128 = (jnp.where(sub4 <= 2, gp[1], one) * jnp.where(sub4 <= 1, gp[2], one)
            * jnp.where(sub4 == 0, gp[3], one))
    px, sx = one, one
    for s in range(1, SUBLANES):
        px = px * jnp.where(sub >= s, gm[s], one)
        sx = sx * jnp.where(sub <= SUBLANES - 1 - s, gp[s], one)
    g_total = (px * g * sx)[0:1, :]

    k_rev = times(k_t, rev)
    q_fwd = times(q_t, fwd) if want_q else None

    def scaled(x_t, s):
        return _cat_bf16([x * s for x in x_t])

    levels.append((_cat_bf16(q_fwd) if want_q else None, _cat_bf16(k_rev)))
    levels.append((scaled(q_fwd, u64) if want_q else None, scaled(k_rev, w64)))
    levels.append((scaled(q_fwd, u128) if want_q else None, scaled(k_rev, w128)))
    levels.append((_cat_bf16(q_t) if want_q else None, _cat_bf16(k_t)))
    q_inter = scaled(q_fwd, px) if want_q else None
    k_state = scaled(k_rev, sx)
    return levels, q_inter, k_state, g_total


def _norm_and_permute(x_rows, norm_g_ref, pmat_ref, h_nat, h_p):
    def body(i):
        rows = _rows(i, ROW_TILE)
        xv = x_rows(rows)
        ms = jnp.mean(xv * xv, axis=-1, keepdims=True)
        h_nat[rows, :] = (xv * lax.rsqrt(ms + EPS) * norm_g_ref[...]).astype(BF16)
    _tile_loop(BLOCK_T // ROW_TILE, body)
    h_p[...] = jnp.dot(pmat_ref[...], h_nat[...], preferred_element_type=F32).astype(BF16)


def _glu_and_history(glu_a, glu_b, carry, a_ext):
    def body(i):
        rows = _rows(i, SUBLANES)
        a = glu_a[rows, :] * _sigmoid(glu_b[rows, :])
        rolled = pltpu.roll(a, 1, 0)
        if a_ext is not None:
            sub = lax.broadcasted_iota(jnp.int32, a.shape, 0)
            a_ext[_rows(i + N_POS, SUBLANES), :] = a
            a_ext[rows, :] = jnp.where(sub == 0, carry[rows, :], rolled)
        carry[rows, :] = rolled
    _tile_loop(N_POS, body)


def _causal_conv(a_ext, conv_w_ref, conv_b_ref, out):
    span = CONV_ROWS + SUBLANES * (CONV_WIDTH - 1)
    for i in range(BLOCK_T // CONV_ROWS):
        first = i * CONV_ROWS + SUBLANES * (N_POS - CONV_WIDTH + 1)
        for l in range(D_MODEL // HEAD_DIM):
            lanes = slice(l * HEAD_DIM, (l + 1) * HEAD_DIM)
            win = a_ext[first:first + span, lanes]
            acc = jnp.broadcast_to(conv_b_ref[:, lanes], (CONV_ROWS, HEAD_DIM))
            for j in range(CONV_WIDTH):
                acc = acc + conv_w_ref[j:j + 1, lanes] * win[SUBLANES * j:SUBLANES * j + CONV_ROWS]
            out[_rows(i, CONV_ROWS), lanes] = acc


def _recurrence_inputs(h_p, w_in_ref, lb_ref, buf, qh, fh, vh, pad_front):
    l0, l1 = lb_ref[0:1, :], lb_ref[1:2, :]
    lmax = jnp.maximum(l0, l1)
    e0, e1 = jnp.exp(l0 - lmax), jnp.exp(l1 - lmax)
    lb = e0 / (e0 + e1)
    n_tiles = BLOCK_T // ROW_TILE

    def to_heads(dst, val, rows):
        for h in range(HEADS):
            dst[h, rows, :] = val[:, h * HEAD_DIM:(h + 1) * HEAD_DIM]

    if qh is not None:
        buf[...] = _proj(h_p, w_in_ref, _C_Q)
        _tile_loop(n_tiles, lambda i: to_heads(qh, _silu(buf[_rows(i, ROW_TILE), :]),
                                               _rows(i, ROW_TILE)))

    buf[...] = _proj(h_p, w_in_ref, _C_F)

    def f_body(i):
        rows = _rows(i, ROW_TILE)
        fv = lb + (1.0 - lb) * _sigmoid(buf[rows, :])
        if pad_front:
            row = lax.broadcasted_iota(jnp.int32, fv.shape, 0) + i * ROW_TILE
            token = N_POS * (row % SUBLANES) + row // SUBLANES
            fv = jnp.where(token < BLOCK_T - N_META, 1.0, fv)
        to_heads(fh, fv, rows)
    _tile_loop(n_tiles, f_body)

    buf[...] = _proj(h_p, w_in_ref, _C_I)
    _tile_loop(n_tiles, lambda i: to_heads(vh, buf[_rows(i, ROW_TILE), :].astype(BF16),
                                           _rows(i, ROW_TILE)))


def _recurrence_head(h, qh, fh, vh, st, lvl_ref, oh):
    want_q = qh is not None
    f_t = [fh[h, _rows(a, SUBLANES), :] for a in range(N_POS)]
    k_t = [1.0 - f for f in f_t]
    q_t = [qh[h, _rows(a, SUBLANES), :] for a in range(N_POS)] if want_q else None
    levels, q_inter, k_state, g_total = _head_levels(q_t, k_t, f_t, want_q)
    v = vh[h]
    upd = lax.dot_general(v, k_state, TN_DIMS, preferred_element_type=F32)
    if not want_q:
        st[h] = upd
        return
    s_t = st[h]
    lvl = lvl_ref[...]
    scores = jnp.zeros((BLOCK_T, BLOCK_T), BF16)
    for l, (ql, kl) in enumerate(levels):
        part = lax.dot_general(ql, kl, NT_DIMS, preferred_element_type=F32)
        scores = jnp.where(lvl == l, part.astype(BF16), scores)
    o = jnp.dot(scores, v, preferred_element_type=F32)
    o = o + lax.dot_general(q_inter, s_t.astype(BF16), NT_DIMS, preferred_element_type=F32)
    oh[h] = o
    st[h] = s_t * g_total + upd


def _meta_kernel(x_ref, pmat_ref, norm_g_ref, w_in_ref, lb_ref, carry_out, st_out,
                 h_nat, h_p, buf0, buf1, fh, vh):
    _norm_and_permute(lambda rows: x_ref[rows, :], norm_g_ref, pmat_ref, h_nat, h_p)
    buf0[...] = _proj(h_p, w_in_ref, _C_GLU_A)
    buf1[...] = _proj(h_p, w_in_ref, _C_GLU_B)
    _glu_and_history(buf0, buf1, carry_out, None)
    _recurrence_inputs(h_p, w_in_ref, lb_ref, buf1, None, fh, vh, pad_front=True)
    for h in range(HEADS):
        _recurrence_head(h, None, fh, vh, st_out, None, None)


def _main_kernel(x_ref, mcarry_ref, mst_ref, pmat_ref, pmat_t_ref, lvl_ref, norm_g_ref,
                 w_in_ref, conv_w_ref, conv_b_ref, ln_g_ref, ln_b_ref, wco_ref, lb_ref, gn_ref,
                 wro_ref, wo_ref, fg_ref, out_ref,
                 h_nat, h_p, conv_out, qh, fh, vh,
                 buf0, buf1, buf2, buf3, buf_z, buf_mc, buf_g, buf_mr,
                 a_ext, carry, st, oh, u_bf):
    @pl.when(pl.program_id(1) == 0)
    def _():
        carry[...] = mcarry_ref[...]
        st[...] = mst_ref[...]

    n_tiles = BLOCK_T // ROW_TILE

    _norm_and_permute(lambda rows: x_ref[0, rows, :], norm_g_ref, pmat_ref, h_nat, h_p)
    buf0[...] = _proj(h_p, w_in_ref, _C_GLU_A)
    buf1[...] = _proj(h_p, w_in_ref, _C_GLU_B)
    _glu_and_history(buf0, buf1, carry, a_ext)
    _causal_conv(a_ext, conv_w_ref, conv_b_ref, conv_out)
    _recurrence_inputs(h_p, w_in_ref, lb_ref, buf1, qh, fh, vh, pad_front=False)

    for h in range(HEADS):
        _recurrence_head(h, qh, fh, vh, st, lvl_ref, oh)

    buf_z[...] = _proj(h_p, w_in_ref, _C_Z)
    buf_mc[...] = _proj(h_p, w_in_ref, _C_MC)
    buf_g[...] = _proj(h_p, w_in_ref, _C_G)
    buf_mr[...] = _proj(h_p, w_in_ref, _C_MR)

    def ln_body(i):
        rows = _rows(i, ROW_TILE)
        cv = conv_out[rows, :]
        mu = jnp.mean(cv, axis=-1, keepdims=True)
        cc = cv - mu
        var = jnp.mean(cc * cc, axis=-1, keepdims=True)
        y = cc * lax.rsqrt(var + EPS) * ln_g_ref[...] + ln_b_ref[...]
        u_bf[rows, :] = (_silu(y) * _silu(buf_z[rows, :])).astype(BF16)
    _tile_loop(n_tiles, ln_body)

    buf2[...] = jnp.dot(u_bf[...], wco_ref[...], preferred_element_type=F32)

    def gate_conv_body(i):
        rows = _rows(i, ROW_TILE)
        buf2[rows, :] = _sigmoid(buf_mc[rows, :]) * buf2[rows, :]
    _tile_loop(n_tiles, gate_conv_body)

    def gn_body(i):
        rows = _rows(i, ROW_TILE)
        for h in range(HEADS):
            lanes = slice(h * HEAD_DIM, (h + 1) * HEAD_DIM)
            o = oh[h, rows, :]
            ms = jnp.mean(o * o, axis=-1, keepdims=True)
            on = o * lax.rsqrt(ms + EPS) * gn_ref[:, lanes]
            u_bf[rows, lanes] = (on * _silu(buf_g[rows, lanes])).astype(BF16)
    _tile_loop(n_tiles, gn_body)

    buf3[...] = jnp.dot(u_bf[...], wro_ref[...], preferred_element_type=F32)

    def merge_body(i):
        rows = _rows(i, ROW_TILE)
        merged = buf2[rows, :] + _sigmoid(buf_mr[rows, :]) * buf3[rows, :]
        u_bf[rows, :] = merged.astype(BF16)
    _tile_loop(n_tiles, merge_body)

    h_nat[...] = jnp.dot(pmat_t_ref[...], u_bf[...], preferred_element_type=F32).astype(BF16)
    buf3[...] = jnp.dot(h_nat[...], wo_ref[...], preferred_element_type=F32)

    def out_body(i):
        rows = _rows(i, ROW_TILE)
        y = x_ref[0, rows, :] + buf3[rows, :]
        ms = jnp.mean(y * y, axis=-1, keepdims=True)
        out_ref[0, rows, :] = y * lax.rsqrt(ms + EPS) * fg_ref[...]
    _tile_loop(n_tiles, out_body)


def _layout_constants():
    i = np.arange(BLOCK_T)
    token = N_POS * (i % SUBLANES) + i // SUBLANES
    pmat = np.zeros((BLOCK_T, BLOCK_T), np.float32)
    pmat[i, token] = 1.0
    t, s = token[:, None], token[None, :]
    x = np.bitwise_xor(t, s)
    top = np.floor(np.log2(np.maximum(x, 1))).astype(np.int32)
    lvl = np.where(s < t, top, np.where(s == t, DIAG_LEVEL, -1)).astype(np.int32)
    return pmat, lvl


def _const_spec(shape, n_grid=1):
    zeros = (0,) * len(shape)
    index_map = (lambda n: zeros) if n_grid == 1 else (lambda b, j: zeros)
    return pl.BlockSpec(shape, index_map, pipeline_mode=pl.Buffered(1))


def kernel(x, meta_tokens, norm_g, w_in, conv_w, conv_b, ln_g, ln_b, w_conv_out, lb_logits,
           gnorm_g, w_rec_out, w_out, final_g):
    bsz, seq, d = x.shape
    assert d == D_MODEL and seq % BLOCK_T == 0 and N_META <= BLOCK_T
    assert norm_g.shape[0] == 1, "single-layer block"
    blocks_per_seq = seq // BLOCK_T

    pmat_np, lvl_np = _layout_constants()
    pmat = jnp.asarray(pmat_np, BF16)
    pmat_t = jnp.asarray(pmat_np.T, BF16)
    lvl = jnp.asarray(lvl_np, BF16)

    row = lambda v: v.reshape(1, D_MODEL).astype(F32)
    w_in_b = w_in[0].astype(BF16)
    wco_b = w_conv_out[0].astype(BF16)
    wro_b = w_rec_out[0].astype(BF16)
    wo_b = w_out[0].astype(BF16)
    norm_g2, conv_b2, ln_g2, ln_b2 = row(norm_g[0]), row(conv_b[0]), row(ln_g[0]), row(ln_b[0])
    gn2, fg2 = row(gnorm_g[0]), row(final_g)
    conv_w2 = conv_w[0].astype(F32)
    lb2 = lb_logits.astype(F32)

    meta_block = jnp.concatenate(
        [jnp.zeros((BLOCK_T - N_META, D_MODEL), x.dtype), meta_tokens.astype(x.dtype)], axis=0)

    big = lambda dt: pltpu.VMEM((BLOCK_T, D_MODEL), dt)
    head_major = lambda dt: pltpu.VMEM((HEADS, BLOCK_T, HEAD_DIM), dt)
    params = lambda n_grid: pltpu.CompilerParams(dimension_semantics=("arbitrary",) * n_grid,
                                                 vmem_limit_bytes=VMEM_LIMIT_BYTES)

    meta_carry, meta_state = pl.pallas_call(
        _meta_kernel,
        grid=(1,),
        in_specs=[_const_spec((BLOCK_T, D_MODEL)), _const_spec(pmat.shape),
                  _const_spec(norm_g2.shape), _const_spec(w_in_b.shape), _const_spec(lb2.shape)],
        out_specs=[pl.BlockSpec((BLOCK_T, D_MODEL), lambda n: (0, 0)),
                   pl.BlockSpec((HEADS, HEAD_DIM, HEAD_DIM), lambda n: (0, 0, 0))],
        out_shape=[jax.ShapeDtypeStruct((BLOCK_T, D_MODEL), F32),
                   jax.ShapeDtypeStruct((HEADS, HEAD_DIM, HEAD_DIM), F32)],
        scratch_shapes=[big(BF16), big(BF16), big(F32), big(F32), head_major(F32), head_major(BF16)],
        compiler_params=params(1),
        name="meta_block",
    )(meta_block, pmat, norm_g2, w_in_b, lb2)

    block_spec = pl.BlockSpec((1, BLOCK_T, D_MODEL), lambda b, j: (b, j, 0))
    consts = [meta_carry, meta_state, pmat, pmat_t, lvl, norm_g2, w_in_b, conv_w2, conv_b2,
              ln_g2, ln_b2, wco_b, lb2, gn2, wro_b, wo_b, fg2]
    out = pl.pallas_call(
        _main_kernel,
        grid=(bsz, blocks_per_seq),
        in_specs=[block_spec] + [_const_spec(c.shape, 2) for c in consts],
        out_specs=block_spec,
        out_shape=jax.ShapeDtypeStruct(x.shape, x.dtype),
        scratch_shapes=[big(BF16), big(BF16), big(F32),
                        head_major(F32), head_major(F32), head_major(BF16)]
                       + [big(F32)] * 8
                       + [pltpu.VMEM((HALO_ROWS + BLOCK_T, D_MODEL), F32), big(F32),
                          pltpu.VMEM((HEADS, HEAD_DIM, HEAD_DIM), F32), head_major(F32), big(BF16)],
        compiler_params=params(2),
        name="main_blocks",
    )(x, *consts)
    return out
```

```python
import numpy as np
import jax
import jax.numpy as jnp
from jax import lax
from jax.experimental import pallas as pl
from jax.experimental.pallas import tpu as pltpu

D_MODEL = 1024
N_META = 16
CONV_WIDTH = 31
HEADS = 8
HEAD_DIM = 128
EPS = 1e-6

SUBLANES = 8
BLOCK_T = 256
N_POS = BLOCK_T // SUBLANES
N_LEVELS = 8
DIAG_LEVEL = N_LEVELS
HALO_ROWS = N_POS * SUBLANES
ROW_TILE = 16
PACKED_ROWS = 2 * SUBLANES
CONV_POSITIONS = 4
CONV_TAP_GROUP = 4
VMEM_LIMIT_BYTES = 56 * 1024 * 1024
assert N_POS >= CONV_WIDTH - 1

F32 = jnp.float32
BF16 = jnp.bfloat16
NT_DIMS = (((1,), (1,)), ((), ()))
TN_DIMS = (((0,), (0,)), ((), ()))

_C_GLU_A, _C_GLU_B, _C_Z, _C_Q, _C_F, _C_I, _C_G, _C_MC, _C_MR = [i * D_MODEL for i in range(9)]


def _sigmoid(x):
    return jax.nn.sigmoid(x)


def _silu(x):
    return x * jax.nn.sigmoid(x)


def _rows(i, n):
    return slice(i * n, (i + 1) * n)


def _tile_loop(n, body):
    for i in range(n):
        body(i)


def _proj(h_ref, w_ref, col):
    return jnp.dot(h_ref[...], w_ref[:, col:col + D_MODEL], preferred_element_type=F32)


def _sub_roll(x, shift):
    return pltpu.roll(x, shift % SUBLANES, 0)


def _cat_bf16(tiles):
    return jnp.concatenate(tiles, axis=0).astype(BF16)


def _head_levels(q_t, k_t, f_t, want_q):
    fwd = list(f_t)
    rev = [None] * N_POS
    levels = []

    def times(x_t, s_t):
        return [x if s is None else x * s for x, s in zip(x_t, s_t)]

    n = 1
    while n < N_POS:
        levels.append((_cat_bf16(times(q_t, fwd)) if want_q else None, _cat_bf16(times(k_t, rev))))
        new_fwd, new_rev = list(fwd), list(rev)
        for b0 in range(0, N_POS, 2 * n):
            tot_lo, tot_up = fwd[b0 + n - 1], fwd[b0 + 2 * n - 1]
            for i in range(n):
                new_fwd[b0 + n + i] = fwd[b0 + n + i] * tot_lo
                new_rev[b0 + i] = tot_up if rev[b0 + i] is None else rev[b0 + i] * tot_up
        fwd, rev = new_fwd, new_rev
        n *= 2

    g = fwd[N_POS - 1]
    sub = lax.broadcasted_iota(jnp.int32, g.shape, 0)
    sub4 = sub % 4
    one = jnp.ones_like(g)
    gm = [None] + [_sub_roll(g, s) for s in range(1, SUBLANES)]
    gp = [None] + [_sub_roll(g, -s) for s in range(1, SUBLANES)]
    u64 = jnp.where(sub4 == 3, gm[1], one)
    w64 = jnp.where(sub4 == 0, gp[1], one)
    u128 = (jnp.where(sub4 >= 1, gm[1], one) * jnp.where(sub4 >= 2, gm[2], one)
            * jnp.where(sub4 >= 3, gm[3], one))
    w128 = (jnp.where(sub4 <= 2, gp[1], one) * jnp.where(sub4 <= 1, gp[2], one)
            * jnp.where(sub4 == 0, gp[3], one))
    px, sx = one, one
    for s in range(1, SUBLANES):
        px = px * jnp.where(sub >= s, gm[s], one)
        sx = sx * jnp.where(sub <= SUBLANES - 1 - s, gp[s], one)
    g_total = (px * g * sx)[0:1, :]

    k_rev = times(k_t, rev)
    q_fwd = times(q_t, fwd) if want_q else None

    def scaled(x_t, s):
        return _cat_bf16([x * s for x in x_t])

    levels.append((_cat_bf16(q_fwd) if want_q else None, _cat_bf16(k_rev)))
    levels.append((scaled(q_fwd, u64) if want_q else None, scaled(k_rev, w64)))
    levels.append((scaled(q_fwd, u128) if want_q else None, scaled(k_rev, w128)))
    levels.append((_cat_bf16(q_t) if want_q else None, _cat_bf16(k_t)))
    q_inter = scaled(q_fwd, px) if want_q else None
    k_state = scaled(k_rev, sx)
    return levels, q_inter, k_state, g_total


def _norm_and_permute(x_rows, norm_g_ref, pmat_ref, h_nat, h_p):
    def body(i):
        rows = _rows(i, ROW_TILE)
        xv = x_rows(rows)
        ms = jnp.mean(xv * xv, axis=-1, keepdims=True)
        h_nat[rows, :] = (xv * lax.rsqrt(ms + EPS) * norm_g_ref[...]).astype(BF16)
    _tile_loop(BLOCK_T // ROW_TILE, body)
    h_p[...] = jnp.dot(pmat_ref[...], h_nat[...], preferred_element_type=F32).astype(BF16)


def _glu_and_history(glu_a, glu_b, carry, a_ext):
    def body(i):
        rows = _rows(i, SUBLANES)
        a = glu_a[rows, :] * _sigmoid(glu_b[rows, :])
        rolled = pltpu.roll(a, 1, 0)
        if a_ext is not None:
            sub = lax.broadcasted_iota(jnp.int32, a.shape, 0)
            a_ext[_rows(i + N_POS, PACKED_ROWS), :] = _pack_halves(a)
            a_ext[_rows(i, PACKED_ROWS), :] = _pack_halves(
                jnp.where(sub == 0, carry[rows, :], rolled))
        carry[rows, :] = rolled
    _tile_loop(N_POS, body)


def _pack_halves(x):
    half = x.shape[1] // 2
    return jnp.concatenate([x[:, :half], x[:, half:]], axis=0).astype(BF16)


def _causal_conv(a_ext, conv_w_ref, conv_b_ref, out):
    half = D_MODEL // 2
    rows_out = PACKED_ROWS * CONV_POSITIONS
    for p0 in range(0, N_POS, CONV_POSITIONS):
        first = PACKED_ROWS * (p0 + N_POS - CONV_WIDTH + 1)
        for l in range(half // HEAD_DIM):
            lanes = slice(l * HEAD_DIM, (l + 1) * HEAD_DIM)
            win = a_ext[first:first + PACKED_ROWS * (CONV_POSITIONS + CONV_WIDTH - 1), lanes]
            acc = jnp.concatenate([conv_b_ref[:, lanes]] * CONV_POSITIONS, axis=0)
            for j0 in range(0, CONV_WIDTH, CONV_TAP_GROUP):
                part = None
                for j in range(j0, min(j0 + CONV_TAP_GROUP, CONV_WIDTH)):
                    wj = conv_w_ref[PACKED_ROWS * j:PACKED_ROWS * (j + 1), lanes]
                    term = win[PACKED_ROWS * j:PACKED_ROWS * j + rows_out] * jnp.concatenate(
                        [wj] * CONV_POSITIONS, axis=0)
                    part = term if part is None else part + term
                acc = acc + part.astype(F32)
            for q in range(CONV_POSITIONS):
                rows = _rows(p0 + q, SUBLANES)
                out[rows, lanes] = acc[PACKED_ROWS * q:PACKED_ROWS * q + SUBLANES]
                out[rows, half + l * HEAD_DIM:half + (l + 1) * HEAD_DIM] = (
                    acc[PACKED_ROWS * q + SUBLANES:PACKED_ROWS * (q + 1)])


def _recurrence_inputs(h_p, w_in_ref, lb_ref, buf, qh, fh, vh, pad_front):
    l0, l1 = lb_ref[0:1, :], lb_ref[1:2, :]
    lmax = jnp.maximum(l0, l1)
    e0, e1 = jnp.exp(l0 - lmax), jnp.exp(l1 - lmax)
    lb = e0 / (e0 + e1)
    n_tiles = BLOCK_T // ROW_TILE

    def to_heads(dst, val, rows):
        for h in range(HEADS):
            dst[h, rows, :] = val[:, h * HEAD_DIM:(h + 1) * HEAD_DIM]

    if qh is not None:
        buf[...] = _proj(h_p, w_in_ref, _C_Q)
        _tile_loop(n_tiles, lambda i: to_heads(qh, _silu(buf[_rows(i, ROW_TILE), :]),
                                               _rows(i, ROW_TILE)))

    buf[...] = _proj(h_p, w_in_ref, _C_F)

    def f_body(i):
        rows = _rows(i, ROW_TILE)
        fv = lb + (1.0 - lb) * _sigmoid(buf[rows, :])
        if pad_front:
            row = lax.broadcasted_iota(jnp.int32, fv.shape, 0) + i * ROW_TILE
            token = N_POS * (row % SUBLANES) + row // SUBLANES
            fv = jnp.where(token < BLOCK_T - N_META, 1.0, fv)
        to_heads(fh, fv, rows)
    _tile_loop(n_tiles, f_body)

    buf[...] = _proj(h_p, w_in_ref, _C_I)
    _tile_loop(n_tiles, lambda i: to_heads(vh, buf[_rows(i, ROW_TILE), :].astype(BF16),
                                           _rows(i, ROW_TILE)))


def _recurrence_head(h, qh, fh, vh, st, lvl_ref, oh):
    want_q = qh is not None
    f_t = [fh[h, _rows(a, SUBLANES), :] for a in range(N_POS)]
    k_t = [1.0 - f for f in f_t]
    q_t = [qh[h, _rows(a, SUBLANES), :] for a in range(N_POS)] if want_q else None
    levels, q_inter, k_state, g_total = _head_levels(q_t, k_t, f_t, want_q)
    v = vh[h]
    upd = lax.dot_general(v, k_state, TN_DIMS, preferred_element_type=F32)
    if not want_q:
        st[h] = upd
        return
    s_t = st[h]
    lvl = lvl_ref[...]
    scores = jnp.zeros((BLOCK_T, BLOCK_T), BF16)
    for l, (ql, kl) in enumerate(levels):
        part = lax.dot_general(ql, kl, NT_DIMS, preferred_element_type=F32)
        scores = jnp.where(lvl == l, part.astype(BF16), scores)
    o = jnp.dot(scores, v, preferred_element_type=F32)
    o = o + lax.dot_general(q_inter, s_t.astype(BF16), NT_DIMS, preferred_element_type=F32)
    oh[h] = o
    st[h] = s_t * g_total + upd


def _meta_kernel(x_ref, pmat_ref, norm_g_ref, w_in_ref, lb_ref, carry_out, st_out,
                 h_nat, h_p, buf0, buf1, fh, vh):
    _norm_and_permute(lambda rows: x_ref[rows, :], norm_g_ref, pmat_ref, h_nat, h_p)
    buf0[...] = _proj(h_p, w_in_ref, _C_GLU_A)
    buf1[...] = _proj(h_p, w_in_ref, _C_GLU_B)
    _glu_and_history(buf0, buf1, carry_out, None)
    _recurrence_inputs(h_p, w_in_ref, lb_ref, buf1, None, fh, vh, pad_front=True)
    for h in range(HEADS):
        _recurrence_head(h, None, fh, vh, st_out, None, None)


def _main_kernel(x_ref, mcarry_ref, mst_ref, pmat_ref, pmat_t_ref, lvl_ref, norm_g_ref,
                 w_in_ref, conv_w_ref, conv_b_ref, ln_g_ref, ln_b_ref, wco_ref, lb_ref, gn_ref,
                 wro_ref, wo_ref, fg_ref, out_ref,
                 h_nat, h_p, conv_out, qh, fh, vh,
                 buf0, buf1, buf2, buf3, buf_z, buf_mc, buf_g, buf_mr,
                 a_ext, carry, st, oh, u_bf):
    @pl.when(pl.program_id(1) == 0)
    def _():
        carry[...] = mcarry_ref[...]
        st[...] = mst_ref[...]

    n_tiles = BLOCK_T // ROW_TILE

    _norm_and_permute(lambda rows: x_ref[0, rows, :], norm_g_ref, pmat_ref, h_nat, h_p)
    buf0[...] = _proj(h_p, w_in_ref, _C_GLU_A)
    buf1[...] = _proj(h_p, w_in_ref, _C_GLU_B)
    _glu_and_history(buf0, buf1, carry, a_ext)
    _causal_conv(a_ext, conv_w_ref, conv_b_ref, conv_out)
    _recurrence_inputs(h_p, w_in_ref, lb_ref, buf1, qh, fh, vh, pad_front=False)

    for h in range(HEADS):
        _recurrence_head(h, qh, fh, vh, st, lvl_ref, oh)

    buf_z[...] = _proj(h_p, w_in_ref, _C_Z)
    buf_mc[...] = _proj(h_p, w_in_ref, _C_MC)
    buf_g[...] = _proj(h_p, w_in_ref, _C_G)
    buf_mr[...] = _proj(h_p, w_in_ref, _C_MR)

    def ln_body(i):
        rows = _rows(i, ROW_TILE)
        cv = conv_out[rows, :]
        mu = jnp.mean(cv, axis=-1, keepdims=True)
        cc = cv - mu
        var = jnp.mean(cc * cc, axis=-1, keepdims=True)
        y = cc * lax.rsqrt(var + EPS) * ln_g_ref[...] + ln_b_ref[...]
        u_bf[rows, :] = (_silu(y) * _silu(buf_z[rows, :])).astype(BF16)
    _tile_loop(n_tiles, ln_body)

    buf2[...] = jnp.dot(u_bf[...], wco_ref[...], preferred_element_type=F32)

    def gate_conv_body(i):
        rows = _rows(i, ROW_TILE)
        buf2[rows, :] = _sigmoid(buf_mc[rows, :]) * buf2[rows, :]
    _tile_loop(n_tiles, gate_conv_body)

    def gn_body(i):
        rows = _rows(i, ROW_TILE)
        for h in range(HEADS):
            lanes = slice(h * HEAD_DIM, (h + 1) * HEAD_DIM)
            o = oh[h, rows, :]
            ms = jnp.mean(o * o, axis=-1, keepdims=True)
            on = o * lax.rsqrt(ms + EPS) * gn_ref[:, lanes]
            u_bf[rows, lanes] = (on * _silu(buf_g[rows, lanes])).astype(BF16)
    _tile_loop(n_tiles, gn_body)

    buf3[...] = jnp.dot(u_bf[...], wro_ref[...], preferred_element_type=F32)

    def merge_body(i):
        rows = _rows(i, ROW_TILE)
        merged = buf2[rows, :] + _sigmoid(buf_mr[rows, :]) * buf3[rows, :]
        u_bf[rows, :] = merged.astype(BF16)
    _tile_loop(n_tiles, merge_body)

    h_nat[...] = jnp.dot(pmat_t_ref[...], u_bf[...], preferred_element_type=F32).astype(BF16)
    buf3[...] = jnp.dot(h_nat[...], wo_ref[...], preferred_element_type=F32)

    def out_body(i):
        rows = _rows(i, ROW_TILE)
        y = x_ref[0, rows, :] + buf3[rows, :]
        ms = jnp.mean(y * y, axis=-1, keepdims=True)
        out_ref[0, rows, :] = y * lax.rsqrt(ms + EPS) * fg_ref[...]
    _tile_loop(n_tiles, out_body)


def _layout_constants():
    i = np.arange(BLOCK_T)
    token = N_POS * (i % SUBLANES) + i // SUBLANES
    pmat = np.zeros((BLOCK_T, BLOCK_T), np.float32)
    pmat[i, token] = 1.0
    t, s = token[:, None], token[None, :]
    x = np.bitwise_xor(t, s)
    top = np.floor(np.log2(np.maximum(x, 1))).astype(np.int32)
    lvl = np.where(s < t, top, np.where(s == t, DIAG_LEVEL, -1)).astype(np.int32)
    return pmat, lvl


def _const_spec(shape, n_grid=1):
    zeros = (0,) * len(shape)
    index_map = (lambda n: zeros) if n_grid == 1 else (lambda b, j: zeros)
    return pl.BlockSpec(shape, index_map, pipeline_mode=pl.Buffered(1))


def kernel(x, meta_tokens, norm_g, w_in, conv_w, conv_b, ln_g, ln_b, w_conv_out, lb_logits,
           gnorm_g, w_rec_out, w_out, final_g):
    bsz, seq, d = x.shape
    assert d == D_MODEL and seq % BLOCK_T == 0 and N_META <= BLOCK_T
    assert norm_g.shape[0] == 1, "single-layer block"
    blocks_per_seq = seq // BLOCK_T

    pmat_np, lvl_np = _layout_constants()
    pmat = jnp.asarray(pmat_np, BF16)
    pmat_t = jnp.asarray(pmat_np.T, BF16)
    lvl = jnp.asarray(lvl_np, BF16)

    row = lambda v: v.reshape(1, D_MODEL).astype(F32)
    w_in_b = w_in[0].astype(BF16)
    wco_b = w_conv_out[0].astype(BF16)
    wro_b = w_rec_out[0].astype(BF16)
    wo_b = w_out[0].astype(BF16)
    norm_g2, conv_b2, ln_g2, ln_b2 = row(norm_g[0]), row(conv_b[0]), row(ln_g[0]), row(ln_b[0])
    gn2, fg2 = row(gnorm_g[0]), row(final_g)
    def halves(v):
        lo, hi = v[..., None, :D_MODEL // 2], v[..., None, D_MODEL // 2:]
        shape = v.shape[:-1] + (SUBLANES, D_MODEL // 2)
        return jnp.concatenate([jnp.broadcast_to(lo, shape), jnp.broadcast_to(hi, shape)], axis=-2)
    conv_w2 = halves(conv_w[0].astype(F32)).reshape(CONV_WIDTH * PACKED_ROWS, D_MODEL // 2).astype(BF16)
    conv_b2 = halves(conv_b[0].astype(F32))
    lb2 = lb_logits.astype(F32)

    meta_block = jnp.concatenate(
        [jnp.zeros((BLOCK_T - N_META, D_MODEL), x.dtype), meta_tokens.astype(x.dtype)], axis=0)

    big = lambda dt: pltpu.VMEM((BLOCK_T, D_MODEL), dt)
    head_major = lambda dt: pltpu.VMEM((HEADS, BLOCK_T, HEAD_DIM), dt)
    params = lambda n_grid: pltpu.CompilerParams(dimension_semantics=("arbitrary",) * n_grid,
                                                 vmem_limit_bytes=VMEM_LIMIT_BYTES)

    meta_carry, meta_state = pl.pallas_call(
        _meta_kernel,
        grid=(1,),
        in_specs=[_const_spec((BLOCK_T, D_MODEL)), _const_spec(pmat.shape),
                  _const_spec(norm_g2.shape), _const_spec(w_in_b.shape), _const_spec(lb2.shape)],
        out_specs=[pl.BlockSpec((BLOCK_T, D_MODEL), lambda n: (0, 0)),
                   pl.BlockSpec((HEADS, HEAD_DIM, HEAD_DIM), lambda n: (0, 0, 0))],
        out_shape=[jax.ShapeDtypeStruct((BLOCK_T, D_MODEL), F32),
                   jax.ShapeDtypeStruct((HEADS, HEAD_DIM, HEAD_DIM), F32)],
        scratch_shapes=[big(BF16), big(BF16), big(F32), big(F32), head_major(F32), head_major(BF16)],
        compiler_params=params(1),
        name="meta_block",
    )(meta_block, pmat, norm_g2, w_in_b, lb2)

    block_spec = pl.BlockSpec((1, BLOCK_T, D_MODEL), lambda b, j: (b, j, 0))
    consts = [meta_carry, meta_state, pmat, pmat_t, lvl, norm_g2, w_in_b, conv_w2, conv_b2,
              ln_g2, ln_b2, wco_b, lb2, gn2, wro_b, wo_b, fg2]
    out = pl.pallas_call(
        _main_kernel,
        grid=(bsz, blocks_per_seq),
        in_specs=[block_spec] + [_const_spec(c.shape, 2) for c in consts],
        out_specs=block_spec,
        out_shape=jax.ShapeDtypeStruct(x.shape, x.dtype),
        scratch_shapes=[big(BF16), big(BF16), big(F32),
                        head_major(F32), head_major(F32), head_major(BF16)]
                       + [big(F32)] * 8
                       + [pltpu.VMEM((2 * N_POS * PACKED_ROWS, D_MODEL // 2), BF16), big(F32),
                          pltpu.VMEM((HEADS, HEAD_DIM, HEAD_DIM), F32), head_major(F32), big(BF16)],
        compiler_params=params(2),
        name="main_blocks",
    )(x, *consts)
    return out
```

```python
import numpy as np
import jax
import jax.numpy as jnp
from jax import lax
from jax.experimental import pallas as pl
from jax.experimental.pallas import tpu as pltpu

D_MODEL = 1024
N_META = 16
CONV_WIDTH = 31
HEADS = 8
HEAD_DIM = 128
EPS = 1e-6

SUBLANES = 8
BLOCK_T = 256
N_POS = BLOCK_T // SUBLANES
N_LEVELS = 8
DIAG_LEVEL = N_LEVELS
HALO_ROWS = N_POS * SUBLANES
ROW_TILE = 16
CONV_ROWS = 32
VMEM_LIMIT_BYTES = 56 * 1024 * 1024
assert N_POS >= CONV_WIDTH - 1

F32 = jnp.float32
BF16 = jnp.bfloat16
NT_DIMS = (((1,), (1,)), ((), ()))
TN_DIMS = (((0,), (0,)), ((), ()))

_C_GLU_A, _C_GLU_B, _C_Z, _C_Q, _C_F, _C_I, _C_G, _C_MC, _C_MR = [i * D_MODEL for i in range(9)]


def _sigmoid(x):
    return jax.nn.sigmoid(x)


def _silu(x):
    return x * jax.nn.sigmoid(x)


def _rows(i, n):
    return slice(i * n, (i + 1) * n)


def _tile_loop(n, body):
    for i in range(n):
        body(i)


def _proj(h_ref, w_ref, col):
    return jnp.dot(h_ref[...], w_ref[:, col:col + D_MODEL], preferred_element_type=F32)


def _sub_roll(x, shift):
    return pltpu.roll(x, shift % SUBLANES, 0)


def _cat_bf16(tiles):
    return jnp.concatenate(tiles, axis=0).astype(BF16)


def _head_levels(q_t, k_t, f_t, want_q):
    fwd = list(f_t)
    rev = [None] * N_POS
    levels = []

    def times(x_t, s_t):
        return [x if s is None else x * s for x, s in zip(x_t, s_t)]

    n = 1
    while n < N_POS:
        levels.append((_cat_bf16(times(q_t, fwd)) if want_q else None, _cat_bf16(times(k_t, rev))))
        new_fwd, new_rev = list(fwd), list(rev)
        for b0 in range(0, N_POS, 2 * n):
            tot_lo, tot_up = fwd[b0 + n - 1], fwd[b0 + 2 * n - 1]
            for i in range(n):
                new_fwd[b0 + n + i] = fwd[b0 + n + i] * tot_lo
                new_rev[b0 + i] = tot_up if rev[b0 + i] is None else rev[b0 + i] * tot_up
        fwd, rev = new_fwd, new_rev
        n *= 2

    g = fwd[N_POS - 1]
    sub = lax.broadcasted_iota(jnp.int32, g.shape, 0)
    sub4 = sub % 4
    one = jnp.ones_like(g)
    gm = [None] + [_sub_roll(g, s) for s in range(1, SUBLANES)]
    gp = [None] + [_sub_roll(g, -s) for s in range(1, SUBLANES)]
    u64 = jnp.where(sub4 == 3, gm[1], one)
    w64 = jnp.where(sub4 == 0, gp[1], one)
    u128 = (jnp.where(sub4 >= 1, gm[1], one) * jnp.where(sub4 >= 2, gm[2], one)
            * jnp.where(sub4 >= 3, gm[3], one))
    w128 = (jnp.where(sub4 <= 2, gp[1], one) * jnp.where(sub4 <= 1, gp[2], one)
            * jnp.where(sub4 == 0, gp[3], one))
    px, sx = one, one
    for s in range(1, SUBLANES):
        px = px * jnp.where(sub >= s, gm[s], one)
        sx = sx * jnp.where(sub <= SUBLANES - 1 - s, gp[s], one)
    g_total = (px * g * sx)[0:1, :]

    k_rev = times(k_t, rev)
    q_fwd = times(q_t, fwd) if want_q else None

    def scaled(x_t, s):
        return _cat_bf16([x * s for x in x_t])

    levels.append((_cat_bf16(q_fwd) if want_q else None, _cat_bf16(k_rev)))
    levels.append((scaled(q_fwd, u64) if want_q else None, scaled(k_rev, w64)))
    levels.append((scaled(q_fwd, u128) if want_q else None, scaled(k_rev, w128)))
    q_inter = scaled(q_fwd, px) if want_q else None
    k_state = scaled(k_rev, sx)
    return levels, q_inter, k_state, g_total


def _norm_and_permute(x_rows, norm_g_ref, pmat_ref, h_nat, h_p):
    def body(i):
        rows = _rows(i, ROW_TILE)
        xv = x_rows(rows)
        ms = jnp.mean(xv * xv, axis=-1, keepdims=True)
        h_nat[rows, :] = (xv * lax.rsqrt(ms + EPS) * norm_g_ref[...]).astype(BF16)
    _tile_loop(BLOCK_T // ROW_TILE, body)
    h_p[...] = jnp.dot(pmat_ref[...], h_nat[...], preferred_element_type=F32).astype(BF16)


def _glu_and_history(glu_a, glu_b, carry, a_ext):
    def body(i):
        rows = _rows(i, SUBLANES)
        a = glu_a[rows, :] * _sigmoid(glu_b[rows, :])
        rolled = pltpu.roll(a, 1, 0)
        if a_ext is not None:
            sub = lax.broadcasted_iota(jnp.int32, a.shape, 0)
            a_ext[_rows(i + N_POS, SUBLANES), :] = a
            a_ext[rows, :] = jnp.where(sub == 0, carry[rows, :], rolled)
        carry[rows, :] = rolled
    _tile_loop(N_POS, body)


def _causal_conv(a_ext, conv_w_ref, conv_b_ref, out):
    span = CONV_ROWS + SUBLANES * (CONV_WIDTH - 1)
    for i in range(BLOCK_T // CONV_ROWS):
        first = i * CONV_ROWS + SUBLANES * (N_POS - CONV_WIDTH + 1)
        for l in range(D_MODEL // HEAD_DIM):
            lanes = slice(l * HEAD_DIM, (l + 1) * HEAD_DIM)
            win = a_ext[first:first + span, lanes]
            acc = jnp.broadcast_to(conv_b_ref[:, lanes], (CONV_ROWS, HEAD_DIM))
            for j in range(CONV_WIDTH):
                acc = acc + conv_w_ref[j:j + 1, lanes] * win[SUBLANES * j:SUBLANES * j + CONV_ROWS]
            out[_rows(i, CONV_ROWS), lanes] = acc


def _recurrence_inputs(h_p, w_in_ref, lb_ref, buf, qh, fh, vh, pad_front):
    l0, l1 = lb_ref[0:1, :], lb_ref[1:2, :]
    lmax = jnp.maximum(l0, l1)
    e0, e1 = jnp.exp(l0 - lmax), jnp.exp(l1 - lmax)
    lb = e0 / (e0 + e1)
    n_tiles = BLOCK_T // ROW_TILE

    def to_heads(dst, val, rows):
        for h in range(HEADS):
            dst[h, rows, :] = val[:, h * HEAD_DIM:(h + 1) * HEAD_DIM]

    if qh is not None:
        buf[...] = _proj(h_p, w_in_ref, _C_Q)
        _tile_loop(n_tiles, lambda i: to_heads(qh, _silu(buf[_rows(i, ROW_TILE), :]),
                                               _rows(i, ROW_TILE)))

    buf[...] = _proj(h_p, w_in_ref, _C_F)

    def f_body(i):
        rows = _rows(i, ROW_TILE)
        fv = lb + (1.0 - lb) * _sigmoid(buf[rows, :])
        if pad_front:
            row = lax.broadcasted_iota(jnp.int32, fv.shape, 0) + i * ROW_TILE
            token = N_POS * (row % SUBLANES) + row // SUBLANES
            fv = jnp.where(token < BLOCK_T - N_META, 1.0, fv)
        to_heads(fh, fv, rows)
    _tile_loop(n_tiles, f_body)

    buf[...] = _proj(h_p, w_in_ref, _C_I)
    _tile_loop(n_tiles, lambda i: to_heads(vh, buf[_rows(i, ROW_TILE), :].astype(BF16),
                                           _rows(i, ROW_TILE)))


def _recurrence_head(h, qh, fh, vh, st, lvl_ref, oh):
    want_q = qh is not None
    f_t = [fh[h, _rows(a, SUBLANES), :] for a in range(N_POS)]
    k_t = [1.0 - f for f in f_t]
    q_t = [qh[h, _rows(a, SUBLANES), :] for a in range(N_POS)] if want_q else None
    levels, q_inter, k_state, g_total = _head_levels(q_t, k_t, f_t, want_q)
    v = vh[h]
    upd = lax.dot_general(v, k_state, TN_DIMS, preferred_element_type=F32)
    if not want_q:
        st[h] = upd
        return
    s_t = st[h]
    lvl = lvl_ref[...]
    scores = jnp.zeros((BLOCK_T, BLOCK_T), BF16)
    for l, (ql, kl) in enumerate(levels):
        part = lax.dot_general(ql, kl, NT_DIMS, preferred_element_type=F32)
        scores = jnp.where(lvl == l, part.astype(BF16), scores)
    o = jnp.dot(scores, v, preferred_element_type=F32)
    o = o + lax.dot_general(q_inter, s_t.astype(BF16), NT_DIMS, preferred_element_type=F32)
    qk = jnp.concatenate([q * k for q, k in zip(q_t, k_t)], axis=0)
    o = o + jnp.sum(qk, axis=-1, keepdims=True) * v.astype(F32)
    oh[h] = o
    st[h] = s_t * g_total + upd


def _meta_kernel(x_ref, pmat_ref, norm_g_ref, w_in_ref, lb_ref, carry_out, st_out,
                 h_nat, h_p, buf0, buf1, fh, vh):
    _norm_and_permute(lambda rows: x_ref[rows, :], norm_g_ref, pmat_ref, h_nat, h_p)
    buf0[...] = _proj(h_p, w_in_ref, _C_GLU_A)
    buf1[...] = _proj(h_p, w_in_ref, _C_GLU_B)
    _glu_and_history(buf0, buf1, carry_out, None)
    _recurrence_inputs(h_p, w_in_ref, lb_ref, buf1, None, fh, vh, pad_front=True)
    for h in range(HEADS):
        _recurrence_head(h, None, fh, vh, st_out, None, None)


def _main_kernel(x_ref, mcarry_ref, mst_ref, pmat_ref, pmat_t_ref, lvl_ref, norm_g_ref,
                 w_in_ref, conv_w_ref, conv_b_ref, ln_g_ref, ln_b_ref, wco_ref, lb_ref, gn_ref,
                 wro_ref, wo_ref, fg_ref, out_ref,
                 h_nat, h_p, conv_out, qh, fh, vh,
                 buf0, buf1, buf2, buf3, buf_z, buf_mc, buf_g, buf_mr,
                 a_ext, carry, st, oh, u_bf):
    @pl.when(pl.program_id(1) == 0)
    def _():
        carry[...] = mcarry_ref[...]
        st[...] = mst_ref[...]

    n_tiles = BLOCK_T // ROW_TILE

    _norm_and_permute(lambda rows: x_ref[0, rows, :], norm_g_ref, pmat_ref, h_nat, h_p)
    buf0[...] = _proj(h_p, w_in_ref, _C_GLU_A)
    buf1[...] = _proj(h_p, w_in_ref, _C_GLU_B)
    _glu_and_history(buf0, buf1, carry, a_ext)
    _causal_conv(a_ext, conv_w_ref, conv_b_ref, conv_out)
    _recurrence_inputs(h_p, w_in_ref, lb_ref, buf1, qh, fh, vh, pad_front=False)

    for h in range(HEADS):
        _recurrence_head(h, qh, fh, vh, st, lvl_ref, oh)

    buf_z[...] = _proj(h_p, w_in_ref, _C_Z)
    buf_mc[...] = _proj(h_p, w_in_ref, _C_MC)
    buf_g[...] = _proj(h_p, w_in_ref, _C_G)
    buf_mr[...] = _proj(h_p, w_in_ref, _C_MR)

    def ln_body(i):
        rows = _rows(i, ROW_TILE)
        cv = conv_out[rows, :]
        mu = jnp.mean(cv, axis=-1, keepdims=True)
        cc = cv - mu
        var = jnp.mean(cc * cc, axis=-1, keepdims=True)
        y = cc * lax.rsqrt(var + EPS) * ln_g_ref[...] + ln_b_ref[...]
        u_bf[rows, :] = (_silu(y) * _silu(buf_z[rows, :])).astype(BF16)
    _tile_loop(n_tiles, ln_body)

    buf2[...] = jnp.dot(u_bf[...], wco_ref[...], preferred_element_type=F32)

    def gate_conv_body(i):
        rows = _rows(i, ROW_TILE)
        buf2[rows, :] = _sigmoid(buf_mc[rows, :]) * buf2[rows, :]
    _tile_loop(n_tiles, gate_conv_body)

    def gn_body(i):
        rows = _rows(i, ROW_TILE)
        for h in range(HEADS):
            lanes = slice(h * HEAD_DIM, (h + 1) * HEAD_DIM)
            o = oh[h, rows, :]
            ms = jnp.mean(o * o, axis=-1, keepdims=True)
            on = o * lax.rsqrt(ms + EPS) * gn_ref[:, lanes]
            u_bf[rows, lanes] = (on * _silu(buf_g[rows, lanes])).astype(BF16)
    _tile_loop(n_tiles, gn_body)

    buf3[...] = jnp.dot(u_bf[...], wro_ref[...], preferred_element_type=F32)

    def merge_body(i):
        rows = _rows(i, ROW_TILE)
        merged = buf2[rows, :] + _sigmoid(buf_mr[rows, :]) * buf3[rows, :]
        u_bf[rows, :] = merged.astype(BF16)
    _tile_loop(n_tiles, merge_body)

    h_nat[...] = jnp.dot(pmat_t_ref[...], u_bf[...], preferred_element_type=F32).astype(BF16)
    buf3[...] = jnp.dot(h_nat[...], wo_ref[...], preferred_element_type=F32)

    def out_body(i):
        rows = _rows(i, ROW_TILE)
        y = x_ref[0, rows, :] + buf3[rows, :]
        ms = jnp.mean(y * y, axis=-1, keepdims=True)
        out_ref[0, rows, :] = y * lax.rsqrt(ms + EPS) * fg_ref[...]
    _tile_loop(n_tiles, out_body)


def _layout_constants():
    i = np.arange(BLOCK_T)
    token = N_POS * (i % SUBLANES) + i // SUBLANES
    pmat = np.zeros((BLOCK_T, BLOCK_T), np.float32)
    pmat[i, token] = 1.0
    t, s = token[:, None], token[None, :]
    x = np.bitwise_xor(t, s)
    top = np.floor(np.log2(np.maximum(x, 1))).astype(np.int32)
    lvl = np.where(s < t, top, np.where(s == t, DIAG_LEVEL, -1)).astype(np.int32)
    return pmat, lvl


def _const_spec(shape, n_grid=1):
    zeros = (0,) * len(shape)
    index_map = (lambda n: zeros) if n_grid == 1 else (lambda b, j: zeros)
    return pl.BlockSpec(shape, index_map, pipeline_mode=pl.Buffered(1))


def kernel(x, meta_tokens, norm_g, w_in, conv_w, conv_b, ln_g, ln_b, w_conv_out, lb_logits,
           gnorm_g, w_rec_out, w_out, final_g):
    bsz, seq, d = x.shape
    assert d == D_MODEL and seq % BLOCK_T == 0 and N_META <= BLOCK_T
    assert norm_g.shape[0] == 1, "single-layer block"
    blocks_per_seq = seq // BLOCK_T

    pmat_np, lvl_np = _layout_constants()
    pmat = jnp.asarray(pmat_np, BF16)
    pmat_t = jnp.asarray(pmat_np.T, BF16)
    lvl = jnp.asarray(lvl_np, BF16)

    row = lambda v: v.reshape(1, D_MODEL).astype(F32)
    w_in_b = w_in[0].astype(BF16)
    wco_b = w_conv_out[0].astype(BF16)
    wro_b = w_rec_out[0].astype(BF16)
    wo_b = w_out[0].astype(BF16)
    norm_g2, conv_b2, ln_g2, ln_b2 = row(norm_g[0]), row(conv_b[0]), row(ln_g[0]), row(ln_b[0])
    gn2, fg2 = row(gnorm_g[0]), row(final_g)
    conv_w2 = conv_w[0].astype(F32)
    lb2 = lb_logits.astype(F32)

    meta_block = jnp.concatenate(
        [jnp.zeros((BLOCK_T - N_META, D_MODEL), x.dtype), meta_tokens.astype(x.dtype)], axis=0)

    big = lambda dt: pltpu.VMEM((BLOCK_T, D_MODEL), dt)
    head_major = lambda dt: pltpu.VMEM((HEADS, BLOCK_T, HEAD_DIM), dt)
    params = lambda n_grid: pltpu.CompilerParams(dimension_semantics=("arbitrary",) * n_grid,
                                                 vmem_limit_bytes=VMEM_LIMIT_BYTES)

    meta_carry, meta_state = pl.pallas_call(
        _meta_kernel,
        grid=(1,),
        in_specs=[_const_spec((BLOCK_T, D_MODEL)), _const_spec(pmat.shape),
                  _const_spec(norm_g2.shape), _const_spec(w_in_b.shape), _const_spec(lb2.shape)],
        out_specs=[pl.BlockSpec((BLOCK_T, D_MODEL), lambda n: (0, 0)),
                   pl.BlockSpec((HEADS, HEAD_DIM, HEAD_DIM), lambda n: (0, 0, 0))],
        out_shape=[jax.ShapeDtypeStruct((BLOCK_T, D_MODEL), F32),
                   jax.ShapeDtypeStruct((HEADS, HEAD_DIM, HEAD_DIM), F32)],
        scratch_shapes=[big(BF16), big(BF16), big(F32), big(F32), head_major(F32), head_major(BF16)],
        compiler_params=params(1),
        name="meta_block",
    )(meta_block, pmat, norm_g2, w_in_b, lb2)

    block_spec = pl.BlockSpec((1, BLOCK_T, D_MODEL), lambda b, j: (b, j, 0))
    consts = [meta_carry, meta_state, pmat, pmat_t, lvl, norm_g2, w_in_b, conv_w2, conv_b2,
              ln_g2, ln_b2, wco_b, lb2, gn2, wro_b, wo_b, fg2]
    out = pl.pallas_call(
        _main_kernel,
        grid=(bsz, blocks_per_seq),
        in_specs=[block_spec] + [_const_spec(c.shape, 2) for c in consts],
        out_specs=block_spec,
        out_shape=jax.ShapeDtypeStruct(x.shape, x.dtype),
        scratch_shapes=[big(BF16), big(BF16), big(F32),
                        head_major(F32), head_major(F32), head_major(BF16)]
                       + [big(F32)] * 8
                       + [pltpu.VMEM((HALO_ROWS + BLOCK_T, D_MODEL), F32), big(F32),
                          pltpu.VMEM((HEADS, HEAD_DIM, HEAD_DIM), F32), head_major(F32), big(BF16)],
        compiler_params=params(2),
        name="main_blocks",
    )(x, *consts)
    return out
```
